```python
import math
import jax, jax.numpy as jnp
from jax import lax
import numpy as np

D_MODEL = 1024
BATCH = 4
SEQ = 8192
DEPTH = 2

N_A_LAYERS = DEPTH // 2
N_B_LAYERS = DEPTH - N_A_LAYERS
HG_HEADS = 8
HG_DK = D_MODEL // HG_HEADS
HG_DV = D_MODEL // HG_HEADS
HG_KEY = HG_HEADS * HG_DK
HG_VAL = HG_HEADS * HG_DV
HG_CHUNK = 64
ATT_HEADS = 8
ATT_HD = D_MODEL // ATT_HEADS
ATT_SCALE = ATT_HD ** -0.5
MOBA_BLOCK = 256
MOBA_TOPK = 3
Q_CHUNK = 16
NUM_BUCKETS = 32
MAX_DISTANCE = 1024
N_GROUPS = 4
EXPERTS_PER_GROUP = 8
N_EXPERTS = N_GROUPS * EXPERTS_PER_GROUP
EXPERT_TOPK = 2
EXPERT_FF = D_MODEL // 2
ROW_BLOCK = 128
EPS = 1e-6

kernel_name = 'hgrn2_moba_yoco_hmoe'


def rms_norm(x, w):
    xf = x.astype(jnp.float32)
    y = xf * lax.rsqrt(jnp.mean(xf * xf, axis=-1, keepdims=True) + EPS)
    return (y * w.astype(jnp.float32)).astype(x.dtype)


def _to_chunks(t, n_heads):
    b, l, _ = t.shape
    return t.reshape(b, l // HG_CHUNK, HG_CHUNK, n_heads, -1).transpose(1, 0, 3, 2, 4)


def _gla_chunk_step(state, inp):
    q, k, v, log_f = inp
    b = jnp.cumsum(log_f, axis=2)
    causal = jnp.tril(jnp.ones((HG_CHUNK, HG_CHUNK), dtype=bool))[:, :, None]
    diff = b[:, :, :, None, :] - b[:, :, None, :, :]
    decay = jnp.where(causal, jnp.exp(jnp.where(causal, diff, 0.0)), 0.0)
    scores = jnp.einsum('bhtsd,bhsd->bhts', q[:, :, :, None, :] * decay, k)
    o = (jnp.einsum('bhts,bhsv->bhtv', scores, v)
         + jnp.einsum('bhtd,bhdv->bhtv', q * jnp.exp(b), state))
    b_last = b[:, :, -1, :]
    k_dec = k * jnp.exp(b_last[:, :, None, :] - b)
    state = jnp.exp(b_last)[..., None] * state + jnp.einsum('bhsd,bhsv->bhdv', k_dec, v)
    return state, o


def hgrn2_mixer(h, w_in, lower_bound, o_norm, w_out):
    bsz, seqlen, _ = h.shape
    proj = h @ w_in
    q, f_logit, i, g = jnp.split(proj, [HG_KEY, 2 * HG_KEY, 2 * HG_KEY + HG_VAL], axis=-1)
    lb = lower_bound.astype(jnp.float32)
    f = lb + (1.0 - lb) * jax.nn.sigmoid(f_logit.astype(jnp.float32))
    log_f = jnp.log(f)
    k = 1.0 - f
    qs = _to_chunks(q.astype(jnp.float32) * HG_DK ** -0.5, HG_HEADS)
    ks = _to_chunks(k, HG_HEADS)
    vs = _to_chunks(i.astype(jnp.float32), HG_HEADS)
    gs = _to_chunks(log_f, HG_HEADS)
    state0 = jnp.zeros((bsz, HG_HEADS, HG_DK, HG_DV), jnp.float32)
    _, o = lax.scan(_gla_chunk_step, state0, (qs, ks, vs, gs))
    o = o.transpose(1, 0, 3, 2, 4).reshape(bsz, seqlen, HG_HEADS, HG_DV)
    gate = jax.nn.silu(g.astype(jnp.float32).reshape(bsz, seqlen, HG_HEADS, HG_DV))
    o = rms_norm(o, o_norm) * gate
    return o.reshape(bsz, seqlen, HG_VAL).astype(h.dtype) @ w_out


def shared_kv(mem, kv_norm, w_kv, k_norm):
    bsz, seqlen, _ = mem.shape
    n_blocks = -(-seqlen // MOBA_BLOCK)
    pad = n_blocks * MOBA_BLOCK - seqlen
    kv = rms_norm(mem, kv_norm) @ w_kv
    k, v = jnp.split(kv, 2, axis=-1)
    k = rms_norm(k.reshape(bsz, seqlen, ATT_HEADS, ATT_HD), k_norm)
    v = v.reshape(bsz, seqlen, ATT_HEADS, ATT_HD)

    def to_blocks(t):
        t = jnp.pad(t, ((0, 0), (0, pad), (0, 0), (0, 0)))
        return t.reshape(bsz, n_blocks, MOBA_BLOCK, ATT_HEADS, ATT_HD).transpose(0, 3, 1, 2, 4)

    k_blocks = to_blocks(k)
    v_blocks = to_blocks(v)
    k_mean = jnp.mean(k_blocks.astype(jnp.float32), axis=3).astype(k.dtype)
    return k_blocks, v_blocks, k_mean


def t5_bucket(dist):
    n = jnp.maximum(dist, 0)
    max_exact = NUM_BUCKETS // 2
    log_ratio = (jnp.log(jnp.maximum(n, max_exact).astype(jnp.float32) / max_exact)
                 / math.log(MAX_DISTANCE / max_exact))
    large = max_exact + (log_ratio * (NUM_BUCKETS - max_exact)).astype(jnp.int32)
    large = jnp.minimum(large, NUM_BUCKETS - 1)
    return jnp.where(n < max_exact, n, large)


def moba_mixer(h, w_q, q_norm, w_o, k_blocks, v_blocks, k_mean, rel_bias):
    bsz, seqlen, _ = h.shape
    n_blocks = k_blocks.shape[2]
    topk = min(MOBA_TOPK, n_blocks)
    q = rms_norm((h @ w_q).reshape(bsz, seqlen, ATT_HEADS, ATT_HD), q_norm)
    q = q.transpose(0, 2, 1, 3)
    bias_tab = rel_bias.astype(jnp.float32)
    b_ix = jnp.arange(bsz)[:, None, None, None]
    h_ix = jnp.arange(ATT_HEADS)[None, :, None, None]
    key_off = jnp.arange(MOBA_BLOCK)

    def attend(c):
        start = c * Q_CHUNK
        blk = start // MOBA_BLOCK
        q_c = lax.dynamic_slice_in_dim(q, start, Q_CHUNK, axis=2)
        q_pos = start + jnp.arange(Q_CHUNK)
        gate = jnp.einsum('bhqd,bhnd->bhqn', q_c, k_mean, preferred_element_type=jnp.float32)
        gate = jnp.where(jnp.arange(n_blocks) < blk, gate, -jnp.inf)
        _, sel = lax.top_k(gate, topk)
        valid = jnp.arange(topk) < blk
        k_sel = k_blocks[b_ix, h_ix, sel]
        v_sel = v_blocks[b_ix, h_ix, sel]
        s_sel = jnp.einsum('bhqd,bhqnkd->bhqnk', q_c, k_sel,
                           preferred_element_type=jnp.float32) * ATT_SCALE
        dist_sel = q_pos[None, None, :, None, None] - (sel[..., None] * MOBA_BLOCK + key_off)
        s_sel = s_sel + bias_tab[t5_bucket(dist_sel), h_ix[..., None]]
        s_sel = jnp.where(valid[:, None], s_sel, -jnp.inf)
        k_own = lax.dynamic_index_in_dim(k_blocks, blk, axis=2, keepdims=False)
        v_own = lax.dynamic_index_in_dim(v_blocks, blk, axis=2, keepdims=False)
        s_own = jnp.einsum('bhqd,bhkd->bhqk', q_c, k_own,
                           preferred_element_type=jnp.float32) * ATT_SCALE
        dist_own = q_pos[:, None] - (blk * MOBA_BLOCK + key_off)[None, :]
        s_own = s_own + bias_tab[t5_bucket(dist_own)].transpose(2, 0, 1)
        s_own = jnp.where(dist_own >= 0, s_own, -jnp.inf)
        logits = jnp.concatenate([s_sel.reshape(bsz, ATT_HEADS, Q_CHUNK, topk * MOBA_BLOCK), s_own], axis=-1)
        p = jax.nn.softmax(logits, axis=-1).astype(v_blocks.dtype)
        p_sel = p[..., :topk * MOBA_BLOCK].reshape(bsz, ATT_HEADS, Q_CHUNK, topk, MOBA_BLOCK)
        p_own = p[..., topk * MOBA_BLOCK:]
        return (jnp.einsum('bhqnk,bhqnkd->bhqd', p_sel, v_sel)
                + jnp.einsum('bhqk,bhkd->bhqd', p_own, v_own))

    o = lax.map(attend, jnp.arange(seqlen // Q_CHUNK))
    o = o.transpose(1, 0, 3, 2, 4).reshape(bsz, seqlen, ATT_HEADS * ATT_HD)
    return o @ w_o


def hier_moe(h, w_group, b_group, w_expert, b_expert, w_in, w_out):
    bsz, seqlen, d = h.shape
    ht = h.reshape(-1, d)
    n_tok = ht.shape[0]
    g_logits = (ht @ w_group).astype(jnp.float32) + b_group.astype(jnp.float32)
    g_prob = jax.nn.softmax(g_logits, axis=-1)
    g_idx = jnp.argmax(g_logits, axis=-1)
    g_w = jnp.take_along_axis(g_prob, g_idx[:, None], axis=1)
    e_logits = ((ht @ w_expert).astype(jnp.float32) + b_expert.astype(jnp.float32)).reshape(
        n_tok, N_GROUPS, EXPERTS_PER_GROUP)
    e_logits = jnp.take_along_axis(e_logits, g_idx[:, None, None], axis=1)[:, 0]
    top_p, top_i = lax.top_k(jax.nn.softmax(e_logits, axis=-1), EXPERT_TOPK)
    weights = g_w * (top_p / jnp.sum(top_p, axis=-1, keepdims=True))
    expert_id = (g_idx[:, None] * EXPERTS_PER_GROUP + top_i).astype(jnp.int32)
    flat_id = expert_id.reshape(-1)
    flat_w = weights.reshape(-1)
    n_assign = flat_id.shape[0]
    order = jnp.argsort(flat_id)
    sorted_id = flat_id[order]
    token_of = order // EXPERT_TOPK
    counts = jnp.bincount(flat_id, length=N_EXPERTS).astype(jnp.int32)
    padded = (counts + ROW_BLOCK - 1) // ROW_BLOCK * ROW_BLOCK
    seg_start = jnp.cumsum(counts) - counts
    pad_end = jnp.cumsum(padded)
    pad_start = pad_end - padded
    dest = pad_start[sorted_id] + jnp.arange(n_assign, dtype=jnp.int32) - seg_start[sorted_id]
    n_rb = -(-n_assign // ROW_BLOCK) + N_EXPERTS
    cap = n_rb * ROW_BLOCK
    row_token = jnp.zeros((cap,), jnp.int32).at[dest].set(token_of.astype(jnp.int32))
    x_buf = ht[row_token].reshape(n_rb, ROW_BLOCK, d)
    block_expert = jnp.minimum(
        jnp.searchsorted(pad_end, jnp.arange(n_rb, dtype=jnp.int32) * ROW_BLOCK, side='right'),
        N_EXPERTS - 1)

    def expert_block(args):
        xb, e = args
        hu = xb @ w_in[e]
        return (jax.nn.silu(hu[:, :EXPERT_FF]) * hu[:, EXPERT_FF:]) @ w_out[e]

    y_buf = lax.map(expert_block, (x_buf, block_expert)).reshape(cap, d)
    y = y_buf[dest] * flat_w[order][:, None].astype(y_buf.dtype)
    out = jnp.zeros_like(ht).at[token_of].add(y)
    return out.reshape(bsz, seqlen, d)


def setup_inputs(seed: int = 0) -> dict:
    key = jax.random.key(seed)
    ks = jax.random.split(key, 24)

    def nrm(k, shape, scale):
        return jax.random.normal(k, shape, jnp.float32) * scale

    res = (2.0 * DEPTH) ** -0.5
    return {
        'x': nrm(ks[0], (BATCH, SEQ, D_MODEL), 1.0),
        'mix_norm': 1.0 + nrm(ks[1], (DEPTH, D_MODEL), 0.05),
        'ffn_norm': 1.0 + nrm(ks[2], (DEPTH, D_MODEL), 0.05),
        'hg_w_in': nrm(ks[3], (N_A_LAYERS, D_MODEL, 2 * HG_KEY + 2 * HG_VAL), D_MODEL ** -0.5),
        'hg_lb': nrm(ks[4], (N_A_LAYERS + 1, HG_KEY), 0.5),
        'hg_o_norm': 1.0 + nrm(ks[5], (N_A_LAYERS, HG_DV), 0.05),
        'hg_w_out': nrm(ks[6], (N_A_LAYERS, HG_VAL, D_MODEL), HG_VAL ** -0.5 * res),
        'kv_norm': 1.0 + nrm(ks[7], (D_MODEL,), 0.05),
        'w_kv': nrm(ks[8], (D_MODEL, 2 * ATT_HEADS * ATT_HD), D_MODEL ** -0.5),
        'k_norm': 1.0 + nrm(ks[9], (ATT_HD,), 0.05),
        'att_w_q': nrm(ks[10], (N_B_LAYERS, D_MODEL, ATT_HEADS * ATT_HD), D_MODEL ** -0.5),
        'q_norm': 1.0 + nrm(ks[11], (N_B_LAYERS, ATT_HD), 0.05),
        'att_w_o': nrm(ks[12], (N_B_LAYERS, ATT_HEADS * ATT_HD, D_MODEL), (ATT_HEADS * ATT_HD) ** -0.5 * res),
        'rel_bias': nrm(ks[13], (NUM_BUCKETS, ATT_HEADS), 0.5),
        'moe_w_group': nrm(ks[14], (DEPTH, D_MODEL, N_GROUPS), D_MODEL ** -0.5),
        'moe_b_group': nrm(ks[15], (DEPTH, N_GROUPS), 0.01),
        'moe_w_expert': nrm(ks[16], (DEPTH, D_MODEL, N_EXPERTS), D_MODEL ** -0.5),
        'moe_b_expert': nrm(ks[17], (DEPTH, N_EXPERTS), 0.01),
        'moe_w_in': nrm(ks[18], (DEPTH, N_EXPERTS, D_MODEL, 2 * EXPERT_FF), D_MODEL ** -0.5),
        'moe_w_out': nrm(ks[19], (DEPTH, N_EXPERTS, EXPERT_FF, D_MODEL), EXPERT_FF ** -0.5 * res),
    }


def reference(x, mix_norm, ffn_norm, hg_w_in, hg_lb, hg_o_norm, hg_w_out, kv_norm, w_kv, k_norm,
              att_w_q, q_norm, att_w_o, rel_bias, moe_w_group, moe_b_group, moe_w_expert,
              moe_b_expert, moe_w_in, moe_w_out):
    lower_bounds = jnp.cumsum(jax.nn.softmax(hg_lb.astype(jnp.float32), axis=0), axis=0)
    k_blocks = v_blocks = k_mean = None
    for layer in range(DEPTH):
        hmix = rms_norm(x, mix_norm[layer])
        if layer < N_A_LAYERS:
            x = x + hgrn2_mixer(hmix, hg_w_in[layer], lower_bounds[layer], hg_o_norm[layer], hg_w_out[layer])
        else:
            j = layer - N_A_LAYERS
            x = x + moba_mixer(hmix, att_w_q[j], q_norm[j], att_w_o[j], k_blocks, v_blocks, k_mean, rel_bias)
        x = x + hier_moe(rms_norm(x, ffn_norm[layer]), moe_w_group[layer], moe_b_group[layer],
                         moe_w_expert[layer], moe_b_expert[layer], moe_w_in[layer], moe_w_out[layer])
        if layer == N_A_LAYERS - 1:
            k_blocks, v_blocks, k_mean = shared_kv(x, kv_norm, w_kv, k_norm)
    return x
```

```python
import functools
import math

import jax
import jax.numpy as jnp
from jax import lax
from jax.experimental import pallas as pl
from jax.experimental.pallas import tpu as pltpu

F32 = jnp.float32
BF16 = jnp.bfloat16
HIGHEST = lax.Precision.HIGHEST

D_MODEL = 1024
N_HEADS = 8
HEAD_DIM = 128
HG_CHUNK = 64
HG_SUB = 16
ATT_SCALE = HEAD_DIM ** -0.5
MOBA_BLOCK = 256
MOBA_TOPK = 3
NUM_BUCKETS = 32
MAX_DISTANCE = 1024
N_GROUPS = 4
EXPERTS_PER_GROUP = 8
N_EXPERTS = N_GROUPS * EXPERTS_PER_GROUP
EXPERT_FF = D_MODEL // 2
EPS = 1e-6
NEG = -1e30

LANES = 128
ROW_TILE = 256
GLA_TILE = 512
ROW_BLOCK = 256
VMEM_LIMIT = 56 * 1024 * 1024


def _near_block_count():
    max_exact = NUM_BUCKETS // 2
    delta = 1
    while True:
        dist = delta * MOBA_BLOCK - (MOBA_BLOCK - 1)
        steps = (math.log(max(dist, max_exact) / max_exact) / math.log(MAX_DISTANCE / max_exact)
                 * (NUM_BUCKETS - max_exact))
        if steps >= NUM_BUCKETS - 1 - max_exact + 0.5:
            return delta
        delta += 1


N_NEAR = _near_block_count()


def _params(*sem):
    return pltpu.CompilerParams(dimension_semantics=sem, vmem_limit_bytes=VMEM_LIMIT)


def _const_spec(shape):
    nd = len(shape)
    return pl.BlockSpec(shape, lambda *_: (0,) * nd, pipeline_mode=pl.Buffered(1))


def _rms(x, w):
    return x * lax.rsqrt(jnp.mean(x * x, axis=-1, keepdims=True) + EPS) * w


def _sigmoid(x):
    return 1.0 / (1.0 + jnp.exp(-x))


def _dot_nt(a, b):
    return lax.dot_general(a, b, (((1,), (1,)), ((), ())), preferred_element_type=F32)


def _head_norm(x, w):
    segs = [_rms(x[:, h * HEAD_DIM:(h + 1) * HEAD_DIM], w) for h in range(N_HEADS)]
    return jnp.concatenate(segs, axis=-1)


def _hg_proj_kernel(x_ref, nw_ref, lb_ref, w_ref, q_ref, f_ref, v_ref, g_ref):
    h = _rms(x_ref[...], nw_ref[...]).astype(BF16)
    lbl = lb_ref[...]
    e = jnp.exp(lbl - jnp.max(lbl, axis=0, keepdims=True))
    lb = e[0:1] / jnp.sum(e, axis=0, keepdims=True)
    d = D_MODEL
    q = jnp.dot(h, w_ref[:, 0:d], preferred_element_type=F32)
    q_ref[...] = q * (HEAD_DIM ** -0.5)
    fl = jnp.dot(h, w_ref[:, d:2 * d], preferred_element_type=F32)
    f_ref[...] = lb + (1.0 - lb) * _sigmoid(fl)
    v_ref[...] = jnp.dot(h, w_ref[:, 2 * d:3 * d], preferred_element_type=F32)
    g = jnp.dot(h, w_ref[:, 3 * d:4 * d], preferred_element_type=F32)
    g_ref[...] = g * _sigmoid(g)


def _hg_proj(x, nw, hg_lb, w_in):
    t = x.shape[0]
    out = jax.ShapeDtypeStruct((t, D_MODEL), F32)
    row = pl.BlockSpec((ROW_TILE, D_MODEL), lambda i: (i, 0))
    return pl.pallas_call(
        _hg_proj_kernel,
        grid=(t // ROW_TILE,),
        in_specs=[row, _const_spec((1, D_MODEL)), _const_spec(hg_lb.shape), _const_spec(w_in.shape)],
        out_specs=[row, row, row, row],
        out_shape=[out, out, out, out],
        compiler_params=_params("parallel"),
        name="hg_proj",
    )(x, nw, hg_lb, w_in)


def _gla_chunk(q, f, v, st):
    c, s_ = HG_CHUNK, HG_SUB
    logf = jnp.log(f)
    k = 1.0 - f
    tri = (lax.broadcasted_iota(jnp.int32, (c, c), 0) >= lax.broadcasted_iota(jnp.int32, (c, c), 1)).astype(F32)
    b = jnp.dot(tri, logf, precision=HIGHEST, preferred_element_type=F32)
    lane = lax.broadcasted_iota(jnp.int32, (s_, c), 1)
    sub = lax.broadcasted_iota(jnp.int32, (s_, 1), 0)
    rows = []
    for i in range(c // s_):
        r0 = s_ * i
        qi, bi, ki = q[r0:r0 + s_], b[r0:r0 + s_], k[r0:r0 + s_]
        if i == 0:
            arow = jnp.zeros((s_, c), F32)
        else:
            bref = b[r0 - 1:r0]
            qt = qi * jnp.exp(bi - bref)
            kt = k * jnp.exp(jnp.minimum(bref - b, 0.0))
            arow = jnp.where(lane < r0, _dot_nt(qt.astype(BF16), kt.astype(BF16)), 0.0)
        for s in range(s_):
            z = qi * ki[s:s + 1] * jnp.exp(jnp.minimum(bi - bi[s:s + 1], 0.0))
            col = jnp.where(sub >= s, jnp.sum(z, axis=-1, keepdims=True), 0.0)
            arow = jnp.where(lane == r0 + s, col, arow)
        rows.append(arow)
    a = jnp.concatenate(rows, axis=0)
    o = jnp.dot(a.astype(BF16), v.astype(BF16), preferred_element_type=F32)
    o = o + _dot_nt((q * jnp.exp(b)).astype(BF16), st.astype(BF16))
    bl = b[c - 1:c]
    kd = k * jnp.exp(bl - b)
    upd = lax.dot_general(v.astype(BF16), kd.astype(BF16), (((0,), (0,)), ((), ())), preferred_element_type=F32)
    return o, st * jnp.exp(bl) + upd


def _gla_kernel(q_ref, f_ref, v_ref, g_ref, nw_ref, o_ref, st_ref):
    @pl.when(pl.program_id(2) == 0)
    def _():
        st_ref[...] = jnp.zeros_like(st_ref)

    nw = nw_ref[...]

    def chunk(ci, carry):
        rs = pl.ds(pl.multiple_of(ci * HG_CHUNK, HG_CHUNK), HG_CHUNK)
        o, st = _gla_chunk(q_ref[rs, :], f_ref[rs, :], v_ref[rs, :], st_ref[...])
        st_ref[...] = st
        o_ref[rs, :] = (_rms(o, nw) * g_ref[rs, :]).astype(o_ref.dtype)
        return carry

    lax.fori_loop(0, GLA_TILE // HG_CHUNK, chunk, 0)


def _gla(q, f, v, g, nw, batch, seq):
    t = q.shape[0]
    nl = seq // GLA_TILE
    blk = pl.BlockSpec((GLA_TILE, HEAD_DIM), lambda b, h, l: (b * nl + l, h))
    return pl.pallas_call(
        _gla_kernel,
        grid=(batch, N_HEADS, nl),
        in_specs=[blk, blk, blk, blk, _const_spec((1, HEAD_DIM))],
        out_specs=blk,
        out_shape=jax.ShapeDtypeStruct((t, D_MODEL), BF16),
        scratch_shapes=[pltpu.VMEM((HEAD_DIM, HEAD_DIM), F32)],
        compiler_params=_params("parallel", "parallel", "arbitrary"),
        name="gla",
    )(q, f, v, g, nw)


def _proj_route_kernel(a_ref, x_ref, w_ref, nw_ref, wrh_ref, wrl_ref, br_ref, xo_ref, hn_ref, rt_ref):
    y = x_ref[...] + jnp.dot(a_ref[...], w_ref[...], preferred_element_type=F32)
    xo_ref[...] = y
    hn = _rms(y, nw_ref[...])
    hn_ref[...] = hn
    hh = hn.astype(BF16)
    hl = (hn - hh.astype(F32)).astype(BF16)
    logits = (jnp.dot(hh, wrh_ref[...], preferred_element_type=F32)
              + jnp.dot(hl, wrh_ref[...], preferred_element_type=F32)
              + jnp.dot(hh, wrl_ref[...], preferred_element_type=F32)) + br_ref[...]
    lane = lax.broadcasted_iota(jnp.int32, logits.shape, 1)
    lanef = lane.astype(F32)
    big = float(LANES)
    gl = jnp.where(lane < N_GROUPS, logits, NEG)
    gmax = jnp.max(gl, axis=-1, keepdims=True)
    gidx = jnp.min(jnp.where(gl == gmax, lanef, big), axis=-1, keepdims=True)
    g_w = 1.0 / jnp.sum(jnp.exp(gl - gmax), axis=-1, keepdims=True)
    lo = N_GROUPS + gidx * EXPERTS_PER_GROUP
    el = jnp.where((lanef >= lo) & (lanef < lo + EXPERTS_PER_GROUP), logits, NEG)
    m1 = jnp.max(el, axis=-1, keepdims=True)
    i1 = jnp.min(jnp.where(el == m1, lanef, big), axis=-1, keepdims=True)
    el2 = jnp.where(lanef == i1, NEG, el)
    m2 = jnp.max(el2, axis=-1, keepdims=True)
    i2 = jnp.min(jnp.where(el2 == m2, lanef, big), axis=-1, keepdims=True)
    e2 = jnp.exp(m2 - m1)
    w1 = g_w / (1.0 + e2)
    w2 = g_w * e2 / (1.0 + e2)
    rt = jnp.where(lane == 0, i1 - N_GROUPS, 0.0)
    rt = jnp.where(lane == 1, i2 - N_GROUPS, rt)
    rt = jnp.where(lane == 2, w1, rt)
    rt_ref[...] = jnp.where(lane == 3, w2, rt)


def _proj_route(a, x, w, nw, wr_hi, wr_lo, br):
    t = x.shape[0]
    row = pl.BlockSpec((ROW_TILE, D_MODEL), lambda i: (i, 0))
    rt = pl.BlockSpec((ROW_TILE, LANES), lambda i: (i, 0))
    act = jax.ShapeDtypeStruct((t, D_MODEL), F32)
    return pl.pallas_call(
        _proj_route_kernel,
        grid=(t // ROW_TILE,),
        in_specs=[row, row, _const_spec(w.shape), _const_spec((1, D_MODEL)),
                  _const_spec(wr_hi.shape), _const_spec(wr_lo.shape), _const_spec((1, LANES))],
        out_specs=[row, row, rt],
        out_shape=[act, act, jax.ShapeDtypeStruct((t, LANES), F32)],
        compiler_params=_params("parallel"),
        name="proj_route",
    )(a, x, w, nw, wr_hi, wr_lo, br)


def _router_weights(w_group, b_group, w_expert, b_expert):
    pad = LANES - N_GROUPS - N_EXPERTS
    w = jnp.concatenate([w_group, w_expert, jnp.zeros((D_MODEL, pad), F32)], axis=1)
    b = jnp.concatenate([b_group, b_expert, jnp.zeros((pad,), F32)])[None, :]
    hi = w.astype(BF16)
    lo = (w - hi.astype(F32)).astype(BF16)
    return hi, lo, b


def _row_copy(src, src_row, dst, dst_row, sem):
    return pltpu.make_async_copy(src.at[pl.ds(src_row, 1)], dst.at[pl.ds(dst_row, 1)], sem)


def _dispatch_kernel(dest_ref, hn_hbm, xin_hbm, xbuf_hbm, sem):
    del xin_hbm
    base = pl.program_id(0) * ROW_TILE

    def issue(r, carry):
        for k in range(2):
            _row_copy(hn_hbm, base + r, xbuf_hbm, dest_ref[0, 0, 2 * r + k], sem).start()
        return carry

    def drain(r, carry):
        for k in range(2):
            _row_copy(hn_hbm, 0, xbuf_hbm, 0, sem).wait()
        return carry

    lax.fori_loop(0, ROW_TILE, issue, 0)
    lax.fori_loop(0, ROW_TILE, drain, 0)


def _dispatch(dest3, hn, cap):
    t = hn.shape[0]
    xbuf0 = jnp.zeros((cap, D_MODEL), F32)
    return pl.pallas_call(
        _dispatch_kernel,
        grid=(t // ROW_TILE,),
        in_specs=[pl.BlockSpec((1, 1, 2 * ROW_TILE), lambda i: (i, 0, 0), memory_space=pltpu.SMEM),
                  pl.BlockSpec(memory_space=pl.ANY), pl.BlockSpec(memory_space=pl.ANY)],
        out_specs=pl.BlockSpec(memory_space=pl.ANY),
        out_shape=jax.ShapeDtypeStruct((cap, D_MODEL), F32),
        scratch_shapes=[pltpu.SemaphoreType.DMA],
        input_output_aliases={2: 0},
        compiler_params=pltpu.CompilerParams(dimension_semantics=("arbitrary",), has_side_effects=True),
        name="moe_dispatch",
    )(dest3, hn, xbuf0)


def _experts_kernel(be_ref, nused_ref, x_ref, wi_ref, wo_ref, y_ref):
    del be_ref
    i = pl.program_id(0)

    @pl.when(i < nused_ref[0])
    def _():
        hu = jnp.dot(x_ref[...].astype(BF16), wi_ref[0], preferred_element_type=F32)
        u = hu[:, :EXPERT_FF]
        act = u * _sigmoid(u) * hu[:, EXPERT_FF:]
        y_ref[...] = jnp.dot(act.astype(BF16), wo_ref[0], preferred_element_type=F32)

    @pl.when(i >= nused_ref[0])
    def _():
        y_ref[...] = jnp.zeros_like(y_ref)


def _experts(block_expert, nused, xbuf, w_in, w_out):
    cap = xbuf.shape[0]
    row = pl.BlockSpec((ROW_BLOCK, D_MODEL), lambda i, be, nu: (i, 0))
    grid_spec = pltpu.PrefetchScalarGridSpec(
        num_scalar_prefetch=2,
        grid=(cap // ROW_BLOCK,),
        in_specs=[row,
                  pl.BlockSpec((1, D_MODEL, 2 * EXPERT_FF), lambda i, be, nu: (be[i], 0, 0)),
                  pl.BlockSpec((1, EXPERT_FF, D_MODEL), lambda i, be, nu: (be[i], 0, 0))],
        out_specs=row,
    )
    return pl.pallas_call(
        _experts_kernel,
        grid_spec=grid_spec,
        out_shape=jax.ShapeDtypeStruct((cap, D_MODEL), F32),
        compiler_params=_params("arbitrary"),
        name="moe_experts",
    )(block_expert, nused, xbuf, w_in, w_out)


def _combine_kernel(dest_ref, rt_ref, x_ref, y_hbm, o_ref, buf, sem):
    def issue(r, carry):
        for k in range(2):
            pltpu.make_async_copy(y_hbm.at[pl.ds(dest_ref[0, 0, 2 * r + k], 1)], buf.at[k, pl.ds(r, 1)], sem).start()
        return carry

    def drain(r, carry):
        for k in range(2):
            pltpu.make_async_copy(y_hbm.at[pl.ds(0, 1)], buf.at[k, pl.ds(0, 1)], sem).wait()
        return carry

    lax.fori_loop(0, ROW_TILE, issue, 0)
    lax.fori_loop(0, ROW_TILE, drain, 0)
    rt = rt_ref[...]
    o_ref[...] = x_ref[...] + rt[:, 2:3] * buf[0] + rt[:, 3:4] * buf[1]


def _combine(dest3, route, x, ybuf):
    t = x.shape[0]
    row = pl.BlockSpec((ROW_TILE, D_MODEL), lambda i: (i, 0))
    return pl.pallas_call(
        _combine_kernel,
        grid=(t // ROW_TILE,),
        in_specs=[pl.BlockSpec((1, 1, 2 * ROW_TILE), lambda i: (i, 0, 0), memory_space=pltpu.SMEM),
                  pl.BlockSpec((ROW_TILE, LANES), lambda i: (i, 0)), row,
                  pl.BlockSpec(memory_space=pl.ANY)],
        out_specs=row,
        out_shape=jax.ShapeDtypeStruct((t, D_MODEL), F32),
        scratch_shapes=[pltpu.VMEM((2, ROW_TILE, D_MODEL), F32), pltpu.SemaphoreType.DMA],
        compiler_params=_params("arbitrary"),
        name="moe_combine",
    )(dest3, route, x, ybuf)


def _moe(x, hn, route, w_in, w_out):
    t = x.shape[0]
    n_assign = 2 * t
    n_rb = -(-n_assign // ROW_BLOCK) + N_EXPERTS
    cap = n_rb * ROW_BLOCK
    flat_id = route[:, 0:2].astype(jnp.int32).reshape(-1)
    onehot = (flat_id[:, None] == jnp.arange(N_EXPERTS, dtype=jnp.int32)[None, :]).astype(jnp.int32)
    csum = jnp.cumsum(onehot, axis=0)
    rank = jnp.sum(csum * onehot, axis=1) - 1
    counts = csum[-1]
    padded = (counts + ROW_BLOCK - 1) // ROW_BLOCK * ROW_BLOCK
    pad_end = jnp.cumsum(padded)
    pad_start = pad_end - padded
    dest = (jnp.sum(pad_start[None, :] * onehot, axis=1) + rank).astype(jnp.int32)
    block_expert = jnp.minimum(
        jnp.searchsorted(pad_end, jnp.arange(n_rb, dtype=jnp.int32) * ROW_BLOCK, side="right"),
        N_EXPERTS - 1).astype(jnp.int32)
    nused = (pad_end[-1:] // ROW_BLOCK).astype(jnp.int32)
    dest3 = dest.reshape(t // ROW_TILE, 1, 2 * ROW_TILE)
    xbuf = _dispatch(dest3, hn, cap)
    ybuf = _experts(block_expert, nused, xbuf, w_in, w_out)
    return _combine(dest3, route, x, ybuf)


def _kvq_kernel(x_ref, kvn_ref, qn_ref, wkv_ref, wq_ref, knw_ref, qnw_ref, k_ref, v_ref, q_ref, km_ref):
    x = x_ref[...]
    hkv = _rms(x, kvn_ref[...]).astype(BF16)
    d = D_MODEL
    k = _head_norm(jnp.dot(hkv, wkv_ref[:, 0:d], preferred_element_type=F32), knw_ref[...])
    k_ref[...] = k.astype(k_ref.dtype)
    km_ref[0] = jnp.mean(k, axis=0, keepdims=True)
    v_ref[...] = jnp.dot(hkv, wkv_ref[:, d:2 * d], preferred_element_type=F32).astype(v_ref.dtype)
    hq = _rms(x, qn_ref[...]).astype(BF16)
    q_ref[...] = _head_norm(jnp.dot(hq, wq_ref[...], preferred_element_type=F32), qnw_ref[...])


def _kvq(x, kv_norm, q_prenorm, w_kv, w_q, k_norm, q_norm):
    t = x.shape[0]
    row = pl.BlockSpec((MOBA_BLOCK, D_MODEL), lambda i: (i, 0))
    return pl.pallas_call(
        _kvq_kernel,
        grid=(t // MOBA_BLOCK,),
        in_specs=[row, _const_spec((1, D_MODEL)), _const_spec((1, D_MODEL)), _const_spec(w_kv.shape),
                  _const_spec(w_q.shape), _const_spec((1, HEAD_DIM)), _const_spec((1, HEAD_DIM))],
        out_specs=[row, row, row, pl.BlockSpec((1, 1, D_MODEL), lambda i: (i, 0, 0))],
        out_shape=[jax.ShapeDtypeStruct((t, D_MODEL), BF16), jax.ShapeDtypeStruct((t, D_MODEL), BF16),
                   jax.ShapeDtypeStruct((t, D_MODEL), F32),
                   jax.ShapeDtypeStruct((t // MOBA_BLOCK, 1, D_MODEL), F32)],
        compiler_params=_params("parallel"),
        name="kvq",
    )(x, kv_norm, q_prenorm, w_kv, w_q, k_norm, q_norm)


def _bias_tiles_kernel(rb_ref, o_ref):
    h, delta = pl.program_id(0), pl.program_id(1)
    shape = (MOBA_BLOCK, MOBA_BLOCK)
    dist = delta * MOBA_BLOCK + lax.broadcasted_iota(jnp.int32, shape, 0) - lax.broadcasted_iota(jnp.int32, shape, 1)
    n = jnp.maximum(dist, 0)
    max_exact = NUM_BUCKETS // 2
    log_ratio = (jnp.log(jnp.maximum(n, max_exact).astype(F32) / max_exact)
                 / math.log(MAX_DISTANCE / max_exact))
    large = max_exact + (log_ratio * (NUM_BUCKETS - max_exact)).astype(jnp.int32)
    bucket = jnp.where(n < max_exact, n, jnp.minimum(large, NUM_BUCKETS - 1))
    bias = jnp.zeros(shape, F32)
    for b in range(NUM_BUCKETS):
        bias = jnp.where(bucket == b, rb_ref[b, h], bias)
    o_ref[0, 0] = jnp.where(dist >= 0, bias, NEG)


def _bias_tiles(rel_bias):
    return pl.pallas_call(
        _bias_tiles_kernel,
        grid=(N_HEADS, N_NEAR),
        in_specs=[pl.BlockSpec(memory_space=pltpu.SMEM)],
        out_specs=pl.BlockSpec((1, 1, MOBA_BLOCK, MOBA_BLOCK), lambda h, d: (h, d, 0, 0)),
        out_shape=jax.ShapeDtypeStruct((N_HEADS, N_NEAR, MOBA_BLOCK, MOBA_BLOCK), F32),
        compiler_params=_params("parallel", "parallel"),
        name="bias_tiles",
    )(rel_bias)


def _select_kernel(q_ref, km_ref, o_ref):
    i = pl.program_id(1)
    nb = km_ref.shape[1]
    km = km_ref[0]
    lane = lax.broadcasted_iota(jnp.int32, (MOBA_BLOCK, nb), 1)
    lanef = lane.astype(F32)
    olane = lax.broadcasted_iota(jnp.int32, (MOBA_BLOCK, LANES), 1)
    out = jnp.full((MOBA_BLOCK, LANES), -1.0, F32)
    for h in range(N_HEADS):
        hs = slice(h * HEAD_DIM, (h + 1) * HEAD_DIM)
        gate = lax.dot_general(q_ref[:, hs], km[:, hs], (((1,), (1,)), ((), ())),
                               precision=HIGHEST, preferred_element_type=F32)
        g = jnp.where(lane < i, gate, NEG)
        for r in range(MOBA_TOPK):
            m = jnp.max(g, axis=-1, keepdims=True)
            idx = jnp.min(jnp.where(g == m, lanef, float(nb)), axis=-1, keepdims=True)
            pick = jnp.where(m > 0.5 * NEG, idx, -1.0)
            out = jnp.where(olane == h * 4 + r, pick, out)
            g = jnp.where(lanef == idx, NEG, g)
    o_ref[...] = out


def _select(qn, kmean, batch, nb):
    t = qn.shape[0]
    return pl.pallas_call(
        _select_kernel,
        grid=(batch, nb),
        in_specs=[pl.BlockSpec((MOBA_BLOCK, D_MODEL), lambda b, i: (b * nb + i, 0)),
                  pl.BlockSpec((1, nb, D_MODEL), lambda b, i: (b, 0, 0))],
        out_specs=pl.BlockSpec((MOBA_BLOCK, LANES), lambda b, i: (b * nb + i, 0)),
        out_shape=jax.ShapeDtypeStruct((t, LANES), F32),
        compiler_params=_params("parallel", "parallel"),
        name="moba_select",
    )(qn, kmean)


def _attention_kernel(rb_ref, q_ref, k_ref, v_ref, sel_ref, bias_ref, o_ref):
    h, i = pl.program_id(1), pl.program_id(2)
    blk = MOBA_BLOCK
    q = (q_ref[...] * ATT_SCALE).astype(BF16)
    selblk = sel_ref[...]
    olane = lax.broadcasted_iota(jnp.int32, selblk.shape, 1)
    picks = [jnp.sum(jnp.where(olane == h * 4 + r, selblk, 0.0), axis=-1, keepdims=True)
             for r in range(MOBA_TOPK)]
    c_far = rb_ref[NUM_BUCKETS - 1, h]

    def rows(j):
        return pl.ds(pl.multiple_of(j * blk, blk), blk)

    s = _dot_nt(q, k_ref[rows(i), :]) + bias_ref[0, 0]
    m = jnp.max(s, axis=-1, keepdims=True)
    p = jnp.exp(s - m)
    l = jnp.sum(p, axis=-1, keepdims=True)
    acc = jnp.dot(p.astype(BF16), v_ref[rows(i), :], preferred_element_type=F32)

    def step(j, carry, near):
        m, l, acc = carry
        s = _dot_nt(q, k_ref[rows(j), :])
        s = s + (bias_ref[0, i - j] if near else c_far)
        jf = j.astype(F32)
        chosen = (picks[0] == jf) | (picks[1] == jf) | (picks[2] == jf)
        mc = jnp.maximum(m, jnp.max(s, axis=-1, keepdims=True))
        p = jnp.exp(s - mc)
        alpha = jnp.exp(m - mc)
        l_new = alpha * l + jnp.sum(p, axis=-1, keepdims=True)
        acc_new = alpha * acc + jnp.dot(p.astype(BF16), v_ref[rows(j), :], preferred_element_type=F32)
        return jnp.where(chosen, mc, m), jnp.where(chosen, l_new, l), jnp.where(chosen, acc_new, acc)

    split = jnp.maximum(i - (N_NEAR - 1), 0)
    carry = lax.fori_loop(0, split, functools.partial(step, near=False), (m, l, acc))
    m, l, acc = lax.fori_loop(split, i, functools.partial(step, near=True), carry)
    o_ref[...] = (acc / l).astype(o_ref.dtype)


def _attention(rel_bias, qn, kn, vv, sel, bias, batch, seq):
    t = qn.shape[0]
    nb = seq // MOBA_BLOCK
    qblk = pl.BlockSpec((MOBA_BLOCK, HEAD_DIM), lambda b, h, i: (b * nb + i, h))
    kvblk = pl.BlockSpec((seq, HEAD_DIM), lambda b, h, i: (b, h))
    return pl.pallas_call(
        _attention_kernel,
        grid=(batch, N_HEADS, nb),
        in_specs=[pl.BlockSpec(memory_space=pltpu.SMEM), qblk, kvblk, kvblk,
                  pl.BlockSpec((MOBA_BLOCK, LANES), lambda b, h, i: (b * nb + i, 0)),
                  pl.BlockSpec((1, N_NEAR, MOBA_BLOCK, MOBA_BLOCK), lambda b, h, i: (h, 0, 0, 0))],
        out_specs=qblk,
        out_shape=jax.ShapeDtypeStruct((t, D_MODEL), BF16),
        compiler_params=_params("parallel", "parallel", "arbitrary"),
        name="moba_attention",
    )(rel_bias, qn, kn, vv, sel, bias)


def kernel(x, mix_norm, ffn_norm, hg_w_in, hg_lb, hg_o_norm, hg_w_out, kv_norm, w_kv, k_norm, att_w_q, q_norm, att_w_o, rel_bias, moe_w_group, moe_b_group, moe_w_expert, moe_b_expert, moe_w_in, moe_w_out):
    batch, seq, d = x.shape
    assert d == D_MODEL and seq % GLA_TILE == 0 and seq % MOBA_BLOCK == 0
    assert mix_norm.shape[0] == 2 and hg_w_in.shape[0] == 1 and att_w_q.shape[0] == 1
    x0 = x.reshape(batch * seq, d)
    w_in_e = moe_w_in.astype(BF16)
    w_out_e = moe_w_out.astype(BF16)

    q, f, v, g = _hg_proj(x0, mix_norm[0:1], hg_lb, hg_w_in[0].astype(BF16))
    og = _gla(q, f, v, g, hg_o_norm[0:1], batch, seq)
    x1, hn, route = _proj_route(og, x0, hg_w_out[0].astype(BF16), ffn_norm[0:1],
                                *_router_weights(moe_w_group[0], moe_b_group[0], moe_w_expert[0], moe_b_expert[0]))
    x2 = _moe(x1, hn, route, w_in_e[0], w_out_e[0])

    kn, vv, qn, kmean = _kvq(x2, kv_norm[None, :], mix_norm[1:2], w_kv.astype(BF16), att_w_q[0].astype(BF16),
                             k_norm[None, :], q_norm[0:1])
    nb = seq // MOBA_BLOCK
    sel = _select(qn, kmean.reshape(batch, nb, d), batch, nb)
    att = _attention(rel_bias, qn, kn, vv, sel, _bias_tiles(rel_bias), batch, seq)
    x3, hn, route = _proj_route(att, x2, att_w_o[0].astype(BF16), ffn_norm[1:2],
                                *_router_weights(moe_w_group[1], moe_b_group[1], moe_w_expert[1], moe_b_expert[1]))
    out = _moe(x3, hn, route, w_in_e[1], w_out_e[1])
    return out.reshape(batch, seq, d)
```

```python
import functools
import math

import jax
import jax.numpy as jnp
import numpy as np
from jax import lax
from jax.experimental import pallas as pl
from jax.experimental.pallas import tpu as pltpu

F32 = jnp.float32
BF16 = jnp.bfloat16
HIGHEST = lax.Precision.HIGHEST

D_MODEL = 1024
N_HEADS = 8
HEAD_DIM = 128
HG_CHUNK = 64
ATT_SCALE = HEAD_DIM ** -0.5
MOBA_BLOCK = 256
MOBA_TOPK = 3
NUM_BUCKETS = 32
MAX_DISTANCE = 1024
N_GROUPS = 4
EXPERTS_PER_GROUP = 8
N_EXPERTS = N_GROUPS * EXPERTS_PER_GROUP
EXPERT_FF = D_MODEL // 2
EPS = 1e-6
NEG = -1e30
LOG2E = math.log2(math.e)
ATT_BATCH_GROUP = 4

LANES = 128
ROW_TILE = 256
GLA_TILE = 512
ROW_BLOCK = 256
VMEM_LIMIT = 56 * 1024 * 1024


def _near_block_count():
    max_exact = NUM_BUCKETS // 2
    delta = 1
    while True:
        dist = delta * MOBA_BLOCK - (MOBA_BLOCK - 1)
        steps = (math.log(max(dist, max_exact) / max_exact) / math.log(MAX_DISTANCE / max_exact)
                 * (NUM_BUCKETS - max_exact))
        if steps >= NUM_BUCKETS - 1 - max_exact + 0.5:
            return delta
        delta += 1


N_NEAR = _near_block_count()


def _params(*sem):
    return pltpu.CompilerParams(dimension_semantics=sem, vmem_limit_bytes=VMEM_LIMIT)


def _const_spec(shape):
    nd = len(shape)
    return pl.BlockSpec(shape, lambda *_: (0,) * nd, pipeline_mode=pl.Buffered(1))


def _rms(x, w):
    return x * lax.rsqrt(jnp.mean(x * x, axis=-1, keepdims=True) + EPS) * w


def _sigmoid(x):
    return 1.0 / (1.0 + jnp.exp(-x))


def _dot_nt(a, b):
    return lax.dot_general(a, b, (((1,), (1,)), ((), ())), preferred_element_type=F32)


def _head_norm(x, w):
    segs = [_rms(x[:, h * HEAD_DIM:(h + 1) * HEAD_DIM], w) for h in range(N_HEADS)]
    return jnp.concatenate(segs, axis=-1)


def _hg_proj_kernel(x_ref, nw_ref, lb_ref, w_ref, q_ref, f_ref, v_ref, g_ref):
    h = _rms(x_ref[...], nw_ref[...]).astype(BF16)
    lbl = lb_ref[...]
    e = jnp.exp(lbl - jnp.max(lbl, axis=0, keepdims=True))
    lb = e[0:1] / jnp.sum(e, axis=0, keepdims=True)
    d = D_MODEL
    q = jnp.dot(h, w_ref[:, 0:d], preferred_element_type=F32)
    q_ref[...] = q * (HEAD_DIM ** -0.5)
    fl = jnp.dot(h, w_ref[:, d:2 * d], preferred_element_type=F32)
    f_ref[...] = lb + (1.0 - lb) * _sigmoid(fl)
    v_ref[...] = jnp.dot(h, w_ref[:, 2 * d:3 * d], preferred_element_type=F32)
    g = jnp.dot(h, w_ref[:, 3 * d:4 * d], preferred_element_type=F32)
    g_ref[...] = g * _sigmoid(g)


def _hg_proj(x, nw, hg_lb, w_in):
    t = x.shape[0]
    out = jax.ShapeDtypeStruct((t, D_MODEL), F32)
    row = pl.BlockSpec((ROW_TILE, D_MODEL), lambda i: (i, 0))
    return pl.pallas_call(
        _hg_proj_kernel,
        grid=(t // ROW_TILE,),
        in_specs=[row, _const_spec((1, D_MODEL)), _const_spec(hg_lb.shape), _const_spec(w_in.shape)],
        out_specs=[row, row, row, row],
        out_shape=[out, out, out, out],
        compiler_params=_params("parallel"),
        name="hg_proj",
    )(x, nw, hg_lb, w_in)


def _gla_level_tables():
    c = HG_CHUNK
    t = np.arange(c)[:, None]
    s = np.arange(c)[None, :]
    cum, mask = [s <= t], [s == t]
    h = c // 2
    while h >= 1:
        mid = t // (2 * h) * (2 * h) + h
        cum.append(s <= mid - 1)
        mask.append((t // (2 * h) == s // (2 * h)) & (t % (2 * h) >= h) & (s % (2 * h) < h))
        h //= 2
    cum = np.concatenate(cum, 0).astype(np.float32)
    return np.tile(cum, (1, 3)), np.stack(mask).astype(np.float32)


def _gla_kernel(q_ref, f_ref, v_ref, g_ref, nw_ref, cum_ref, msk_ref, o_ref, st_ref):
    @pl.when(pl.program_id(1) == 0)
    def _():
        st_ref[...] = jnp.zeros_like(st_ref)

    c = HG_CHUNK
    levels = msk_ref.shape[0]
    nw = nw_ref[...]
    heads = [slice(h * HEAD_DIM, (h + 1) * HEAD_DIM) for h in range(N_HEADS)]

    def chunk(ci, carry):
        rs = pl.ds(pl.multiple_of(ci * c, c), c)
        cum = cum_ref[...]
        qs, ks, vs, bbs, scs = [], [], [], [], []
        for hs in heads:
            f = f_ref[rs, hs]
            lf = jnp.log(f) * LOG2E
            l1 = lf.astype(BF16)
            r1 = lf - l1.astype(F32)
            l2 = r1.astype(BF16)
            l3 = (r1 - l2.astype(F32)).astype(BF16)
            bbs.append(jnp.dot(cum, jnp.concatenate([l1, l2, l3], axis=0), preferred_element_type=F32))
            qs.append(q_ref[rs, hs])
            ks.append(1.0 - f)
            vs.append(v_ref[rs, hs].astype(BF16))
        for h in range(N_HEADS):
            q, k, bb = qs[h], ks[h], bbs[h]
            b = bb[0:c]
            a = _dot_nt(q.astype(BF16), k.astype(BF16)) * msk_ref[0]
            for lev in range(1, levels):
                e = jnp.exp2(-jnp.abs(b - bb[lev * c:(lev + 1) * c]))
                a = a + _dot_nt((q * e).astype(BF16), (k * e).astype(BF16)) * msk_ref[lev]
            scs.append(a.astype(BF16))
        outs = []
        for h in range(N_HEADS):
            q, k, v, b = qs[h], ks[h], vs[h], bbs[h][0:c]
            st = st_ref[h]
            o = jnp.dot(scs[h], v, preferred_element_type=F32)
            o = o + _dot_nt((q * jnp.exp2(b)).astype(BF16), st.astype(BF16))
            bl = b[c - 1:c]
            kd = (k * jnp.exp2(bl - b)).astype(BF16)
            upd = lax.dot_general(v, kd, (((0,), (0,)), ((), ())), preferred_element_type=F32)
            st_ref[h] = st * jnp.exp2(bl) + upd
            outs.append(o)
        for h, hs in enumerate(heads):
            o_ref[rs, hs] = (_rms(outs[h], nw) * g_ref[rs, hs]).astype(o_ref.dtype)
        return carry

    lax.fori_loop(0, GLA_TILE // c, chunk, 0)


def _gla(q, f, v, g, nw, batch, seq):
    t = q.shape[0]
    nl = seq // GLA_TILE
    cum, msk = _gla_level_tables()
    blk = pl.BlockSpec((GLA_TILE, D_MODEL), lambda b, l: (b * nl + l, 0))
    return pl.pallas_call(
        _gla_kernel,
        grid=(batch, nl),
        in_specs=[blk, blk, blk, blk, _const_spec((1, HEAD_DIM)), _const_spec(cum.shape), _const_spec(msk.shape)],
        out_specs=blk,
        out_shape=jax.ShapeDtypeStruct((t, D_MODEL), BF16),
        scratch_shapes=[pltpu.VMEM((N_HEADS, HEAD_DIM, HEAD_DIM), F32)],
        compiler_params=_params("parallel", "arbitrary"),
        name="gla",
    )(q, f, v, g, nw, jnp.asarray(cum, BF16), jnp.asarray(msk, F32))


def _proj_route_kernel(a_ref, x_ref, w_ref, nw_ref, wrh_ref, wrl_ref, br_ref, xo_ref, hn_ref, rt_ref):
    y = x_ref[...] + jnp.dot(a_ref[...], w_ref[...], preferred_element_type=F32)
    xo_ref[...] = y
    hn = _rms(y, nw_ref[...])
    hn_ref[...] = hn
    hh = hn.astype(BF16)
    hl = (hn - hh.astype(F32)).astype(BF16)
    logits = (jnp.dot(hh, wrh_ref[...], preferred_element_type=F32)
              + jnp.dot(hl, wrh_ref[...], preferred_element_type=F32)
              + jnp.dot(hh, wrl_ref[...], preferred_element_type=F32)) + br_ref[...]
    lane = lax.broadcasted_iota(jnp.int32, logits.shape, 1)
    lanef = lane.astype(F32)
    big = float(LANES)
    gl = jnp.where(lane < N_GROUPS, logits, NEG)
    gmax = jnp.max(gl, axis=-1, keepdims=True)
    gidx = jnp.min(jnp.where(gl == gmax, lanef, big), axis=-1, keepdims=True)
    g_w = 1.0 / jnp.sum(jnp.exp(gl - gmax), axis=-1, keepdims=True)
    lo = N_GROUPS + gidx * EXPERTS_PER_GROUP
    el = jnp.where((lanef >= lo) & (lanef < lo + EXPERTS_PER_GROUP), logits, NEG)
    m1 = jnp.max(el, axis=-1, keepdims=True)
    i1 = jnp.min(jnp.where(el == m1, lanef, big), axis=-1, keepdims=True)
    el2 = jnp.where(lanef == i1, NEG, el)
    m2 = jnp.max(el2, axis=-1, keepdims=True)
    i2 = jnp.min(jnp.where(el2 == m2, lanef, big), axis=-1, keepdims=True)
    e2 = jnp.exp(m2 - m1)
    w1 = g_w / (1.0 + e2)
    w2 = g_w * e2 / (1.0 + e2)
    rt = jnp.where(lane == 0, i1 - N_GROUPS, 0.0)
    rt = jnp.where(lane == 1, i2 - N_GROUPS, rt)
    rt = jnp.where(lane == 2, w1, rt)
    rt_ref[...] = jnp.where(lane == 3, w2, rt)


def _proj_route(a, x, w, nw, wr_hi, wr_lo, br):
    t = x.shape[0]
    row = pl.BlockSpec((ROW_TILE, D_MODEL), lambda i: (i, 0))
    rt = pl.BlockSpec((ROW_TILE, LANES), lambda i: (i, 0))
    act = jax.ShapeDtypeStruct((t, D_MODEL), F32)
    return pl.pallas_call(
        _proj_route_kernel,
        grid=(t // ROW_TILE,),
        in_specs=[row, row, _const_spec(w.shape), _const_spec((1, D_MODEL)),
                  _const_spec(wr_hi.shape), _const_spec(wr_lo.shape), _const_spec((1, LANES))],
        out_specs=[row, row, rt],
        out_shape=[act, act, jax.ShapeDtypeStruct((t, LANES), F32)],
        compiler_params=_params("parallel"),
        name="proj_route",
    )(a, x, w, nw, wr_hi, wr_lo, br)


def _router_weights(w_group, b_group, w_expert, b_expert):
    pad = LANES - N_GROUPS - N_EXPERTS
    w = jnp.concatenate([w_group, w_expert, jnp.zeros((D_MODEL, pad), F32)], axis=1)
    b = jnp.concatenate([b_group, b_expert, jnp.zeros((pad,), F32)])[None, :]
    hi = w.astype(BF16)
    lo = (w - hi.astype(F32)).astype(BF16)
    return hi, lo, b


def _dispatch_kernel(dest_ref, hn_ref, xin_hbm, xbuf_hbm, sem):
    del xin_hbm

    def issue(r, carry):
        for k in range(2):
            pltpu.make_async_copy(hn_ref.at[pl.ds(r, 1)], xbuf_hbm.at[pl.ds(dest_ref[0, 0, 2 * r + k], 1)], sem).start()
        return carry

    lax.fori_loop(0, ROW_TILE, issue, 0)
    for k in range(2):
        pltpu.make_async_copy(hn_ref, xbuf_hbm.at[pl.ds(0, ROW_TILE)], sem).wait()


def _dispatch(dest3, hn, cap):
    t = hn.shape[0]
    xbuf0 = jnp.zeros((cap, D_MODEL), F32)
    return pl.pallas_call(
        _dispatch_kernel,
        grid=(t // ROW_TILE,),
        in_specs=[pl.BlockSpec((1, 1, 2 * ROW_TILE), lambda i: (i, 0, 0), memory_space=pltpu.SMEM),
                  pl.BlockSpec((ROW_TILE, D_MODEL), lambda i: (i, 0)), pl.BlockSpec(memory_space=pl.ANY)],
        out_specs=pl.BlockSpec(memory_space=pl.ANY),
        out_shape=jax.ShapeDtypeStruct((cap, D_MODEL), F32),
        scratch_shapes=[pltpu.SemaphoreType.DMA],
        input_output_aliases={2: 0},
        compiler_params=pltpu.CompilerParams(dimension_semantics=("arbitrary",), has_side_effects=True),
        name="moe_dispatch",
    )(dest3, hn, xbuf0)


def _experts_kernel(be_ref, nused_ref, x_ref, wi_ref, wo_ref, y_ref):
    del be_ref
    i = pl.program_id(0)

    @pl.when(i < nused_ref[0])
    def _():
        hu = jnp.dot(x_ref[...].astype(BF16), wi_ref[0], preferred_element_type=F32)
        u = hu[:, :EXPERT_FF]
        act = u * _sigmoid(u) * hu[:, EXPERT_FF:]
        y_ref[...] = jnp.dot(act.astype(BF16), wo_ref[0], preferred_element_type=F32)

    @pl.when(i >= nused_ref[0])
    def _():
        y_ref[...] = jnp.zeros_like(y_ref)


def _experts(block_expert, nused, xbuf, w_in, w_out):
    cap = xbuf.shape[0]
    row = pl.BlockSpec((ROW_BLOCK, D_MODEL), lambda i, be, nu: (i, 0))
    grid_spec = pltpu.PrefetchScalarGridSpec(
        num_scalar_prefetch=2,
        grid=(cap // ROW_BLOCK,),
        in_specs=[row,
                  pl.BlockSpec((1, D_MODEL, 2 * EXPERT_FF), lambda i, be, nu: (be[i], 0, 0)),
                  pl.BlockSpec((1, EXPERT_FF, D_MODEL), lambda i, be, nu: (be[i], 0, 0))],
        out_specs=row,
    )
    return pl.pallas_call(
        _experts_kernel,
        grid_spec=grid_spec,
        out_shape=jax.ShapeDtypeStruct((cap, D_MODEL), F32),
        compiler_params=_params("arbitrary"),
        name="moe_experts",
    )(block_expert, nused, xbuf, w_in, w_out)


def _combine_kernel(dest_ref, rt_ref, x_ref, y_hbm, o_ref, buf, sem):
    def issue(r, carry):
        for k in range(2):
            pltpu.make_async_copy(y_hbm.at[pl.ds(dest_ref[0, 0, 2 * r + k], 1)], buf.at[k, pl.ds(r, 1)], sem).start()
        return carry

    lax.fori_loop(0, ROW_TILE, issue, 0)
    for k in range(2):
        pltpu.make_async_copy(y_hbm.at[pl.ds(0, ROW_TILE)], buf.at[k], sem).wait()
    rt = rt_ref[...]
    o_ref[...] = x_ref[...] + rt[:, 2:3] * buf[0] + rt[:, 3:4] * buf[1]


def _combine(dest3, route, x, ybuf):
    t = x.shape[0]
    row = pl.BlockSpec((ROW_TILE, D_MODEL), lambda i: (i, 0))
    return pl.pallas_call(
        _combine_kernel,
        grid=(t // ROW_TILE,),
        in_specs=[pl.BlockSpec((1, 1, 2 * ROW_TILE), lambda i: (i, 0, 0), memory_space=pltpu.SMEM),
                  pl.BlockSpec((ROW_TILE, LANES), lambda i: (i, 0)), row,
                  pl.BlockSpec(memory_space=pl.ANY)],
        out_specs=row,
        out_shape=jax.ShapeDtypeStruct((t, D_MODEL), F32),
        scratch_shapes=[pltpu.VMEM((2, ROW_TILE, D_MODEL), F32), pltpu.SemaphoreType.DMA],
        compiler_params=_params("arbitrary"),
        name="moe_combine",
    )(dest3, route, x, ybuf)


def _moe(x, hn, route, w_in, w_out):
    t = x.shape[0]
    n_assign = 2 * t
    n_rb = -(-n_assign // ROW_BLOCK) + N_EXPERTS
    cap = n_rb * ROW_BLOCK
    flat_id = route[:, 0:2].astype(jnp.int32).reshape(-1)
    onehot = (flat_id[:, None] == jnp.arange(N_EXPERTS, dtype=jnp.int32)[None, :]).astype(jnp.int32)
    csum = jnp.cumsum(onehot, axis=0)
    rank = jnp.sum(csum * onehot, axis=1) - 1
    counts = csum[-1]
    padded = (counts + ROW_BLOCK - 1) // ROW_BLOCK * ROW_BLOCK
    pad_end = jnp.cumsum(padded)
    pad_start = pad_end - padded
    dest = (jnp.sum(pad_start[None, :] * onehot, axis=1) + rank).astype(jnp.int32)
    block_expert = jnp.minimum(
        jnp.searchsorted(pad_end, jnp.arange(n_rb, dtype=jnp.int32) * ROW_BLOCK, side="right"),
        N_EXPERTS - 1).astype(jnp.int32)
    nused = (pad_end[-1:] // ROW_BLOCK).astype(jnp.int32)
    dest3 = dest.reshape(t // ROW_TILE, 1, 2 * ROW_TILE)
    xbuf = _dispatch(dest3, hn, cap)
    ybuf = _experts(block_expert, nused, xbuf, w_in, w_out)
    return _combine(dest3, route, x, ybuf)


def _kvq_kernel(x_ref, kvn_ref, qn_ref, wk_ref, wvt_ref, wqt_ref, knw_ref, qnw_ref, k_ref, vt_ref, qt_ref, km_ref):
    x = x_ref[...]
    hkv = _rms(x, kvn_ref[...]).astype(BF16)
    k = _head_norm(jnp.dot(hkv, wk_ref[...], preferred_element_type=F32), knw_ref[...])
    k_ref[...] = k.astype(k_ref.dtype)
    km_ref[0] = jnp.mean(k, axis=0, keepdims=True)
    vt = _dot_nt(wvt_ref[...], hkv)
    hq = _rms(x, qn_ref[...]).astype(BF16)
    qt = _dot_nt(wqt_ref[...], hq)
    qnw = qnw_ref[...]
    for h in range(N_HEADS):
        hs = slice(h * HEAD_DIM, (h + 1) * HEAD_DIM)
        vt_ref[0, h, 0] = vt[hs].astype(vt_ref.dtype)
        seg = qt[hs]
        qt_ref[0, h, 0] = seg * lax.rsqrt(jnp.mean(seg * seg, axis=0, keepdims=True) + EPS) * qnw


def _kvq(x, kv_norm, q_prenorm, w_kv, w_q, k_norm, q_norm, batch, nb):
    t = x.shape[0]
    row = pl.BlockSpec((MOBA_BLOCK, D_MODEL), lambda i: (i, 0))
    tblk = pl.BlockSpec((1, N_HEADS, 1, HEAD_DIM, MOBA_BLOCK), lambda i: (i // nb, 0, i % nb, 0, 0))
    tshape = (batch, N_HEADS, nb, HEAD_DIM, MOBA_BLOCK)
    w_k = w_kv[:, :D_MODEL].astype(BF16)
    w_vt = w_kv[:, D_MODEL:].T.astype(BF16)
    w_qt = w_q.T.astype(BF16)
    qnw = jnp.broadcast_to(q_norm.reshape(HEAD_DIM, 1), (HEAD_DIM, MOBA_BLOCK))
    wspec = _const_spec((D_MODEL, D_MODEL))
    return pl.pallas_call(
        _kvq_kernel,
        grid=(t // MOBA_BLOCK,),
        in_specs=[row, _const_spec((1, D_MODEL)), _const_spec((1, D_MODEL)), wspec, wspec, wspec,
                  _const_spec((1, HEAD_DIM)), _const_spec((HEAD_DIM, MOBA_BLOCK))],
        out_specs=[row, tblk, tblk, pl.BlockSpec((1, 1, D_MODEL), lambda i: (i, 0, 0))],
        out_shape=[jax.ShapeDtypeStruct((t, D_MODEL), BF16), jax.ShapeDtypeStruct(tshape, BF16),
                   jax.ShapeDtypeStruct(tshape, F32),
                   jax.ShapeDtypeStruct((t // MOBA_BLOCK, 1, D_MODEL), F32)],
        compiler_params=_params("parallel"),
        name="kvq",
    )(x, kv_norm, q_prenorm, w_k, w_vt, w_qt, k_norm, qnw)


def _bias_tiles_kernel(rb_ref, o_ref):
    h, delta = pl.program_id(0), pl.program_id(1)
    shape = (MOBA_BLOCK, MOBA_BLOCK)
    dist = delta * MOBA_BLOCK + lax.broadcasted_iota(jnp.int32, shape, 1) - lax.broadcasted_iota(jnp.int32, shape, 0)
    n = jnp.maximum(dist, 0)
    max_exact = NUM_BUCKETS // 2
    log_ratio = (jnp.log(jnp.maximum(n, max_exact).astype(F32) / max_exact)
                 / math.log(MAX_DISTANCE / max_exact))
    large = max_exact + (log_ratio * (NUM_BUCKETS - max_exact)).astype(jnp.int32)
    bucket = jnp.where(n < max_exact, n, jnp.minimum(large, NUM_BUCKETS - 1))
    bias = jnp.zeros(shape, F32)
    for b in range(NUM_BUCKETS):
        bias = jnp.where(bucket == b, rb_ref[b, h], bias)
    o_ref[0, 0] = jnp.where(dist >= 0, bias * LOG2E, NEG)


def _bias_tiles(rel_bias):
    return pl.pallas_call(
        _bias_tiles_kernel,
        grid=(N_HEADS, N_NEAR),
        in_specs=[pl.BlockSpec(memory_space=pltpu.SMEM)],
        out_specs=pl.BlockSpec((1, 1, MOBA_BLOCK, MOBA_BLOCK), lambda h, d: (h, d, 0, 0)),
        out_shape=jax.ShapeDtypeStruct((N_HEADS, N_NEAR, MOBA_BLOCK, MOBA_BLOCK), F32),
        compiler_params=_params("parallel", "parallel"),
        name="bias_tiles",
    )(rel_bias)


def _select_kernel(qt_ref, km_ref, o_ref):
    i = pl.program_id(1)
    nb = km_ref.shape[1]
    km = km_ref[0]
    blk = lax.broadcasted_iota(jnp.int32, (nb, MOBA_BLOCK), 0)
    blkf = blk.astype(F32)
    for h in range(N_HEADS):
        gate = jnp.dot(km[:, h * HEAD_DIM:(h + 1) * HEAD_DIM], qt_ref[0, h, 0],
                       precision=HIGHEST, preferred_element_type=F32)
        g = jnp.where(blk < i, gate, NEG)
        mask = jnp.full((nb, MOBA_BLOCK), NEG, F32)
        for _ in range(MOBA_TOPK):
            m = jnp.max(g, axis=0, keepdims=True)
            idx = jnp.min(jnp.where(g == m, blkf, float(nb)), axis=0, keepdims=True)
            hit = (blkf == idx) & (m > 0.5 * NEG)
            mask = jnp.where(hit, 0.0, mask)
            g = jnp.where(blkf == idx, NEG, g)
        o_ref[0, h, 0] = mask


def _select(qt, kmean, batch, nb):
    tblk = pl.BlockSpec((1, N_HEADS, 1, HEAD_DIM, MOBA_BLOCK), lambda b, i: (b, 0, i, 0, 0))
    return pl.pallas_call(
        _select_kernel,
        grid=(batch, nb),
        in_specs=[tblk, pl.BlockSpec((1, nb, D_MODEL), lambda b, i: (b, 0, 0))],
        out_specs=pl.BlockSpec((1, N_HEADS, 1, nb, MOBA_BLOCK), lambda b, i: (b, 0, i, 0, 0)),
        out_shape=jax.ShapeDtypeStruct((batch, N_HEADS, nb, nb, MOBA_BLOCK), F32),
        compiler_params=_params("parallel", "parallel"),
        name="moba_select",
    )(qt, kmean)


def _attention_kernel(rb_ref, qt_ref, k_ref, vt_ref, mask_ref, bias_ref, o_ref, acc_ref):
    g, h, i = pl.program_id(0), pl.program_id(1), pl.program_id(2)
    del g
    blk = MOBA_BLOCK
    nbatch = qt_ref.shape[0]
    qts = [(qt_ref[b, 0, 0] * (ATT_SCALE * LOG2E)).astype(BF16) for b in range(nbatch)]
    c_far = rb_ref[NUM_BUCKETS - 1, h] * LOG2E

    def scores(b, j):
        return jnp.dot(k_ref[b, pl.ds(pl.multiple_of(j * blk, blk), blk), :], qts[b], preferred_element_type=F32)

    ms, ls, ps = [], [], []
    ss = [scores(b, i) for b in range(nbatch)]
    for b in range(nbatch):
        s = ss[b] + bias_ref[0, 0]
        m = jnp.max(s, axis=0, keepdims=True)
        p = jnp.exp2(s - m)
        ms.append(m)
        ls.append(jnp.sum(p, axis=0, keepdims=True))
        ps.append(p.astype(BF16))
    for b in range(nbatch):
        acc_ref[b] = jnp.dot(vt_ref[b, 0, i], ps[b], preferred_element_type=F32)

    def step(j, carry, near):
        ms, ls = carry
        new_m, new_l, ps, alphas = [], [], [], []
        ss = [scores(b, j) for b in range(nbatch)]
        for b in range(nbatch):
            s = ss[b]
            rowb = mask_ref[b, 0, 0, pl.ds(j, 1), :]
            if near:
                s = s + bias_ref[0, i - j]
            else:
                rowb = rowb + c_far
            mc = jnp.maximum(ms[b], jnp.max(s, axis=0, keepdims=True) + rowb)
            p = jnp.exp2(s - (mc - rowb))
            alpha = jnp.exp2(ms[b] - mc)
            new_m.append(mc)
            new_l.append(alpha * ls[b] + jnp.sum(p, axis=0, keepdims=True))
            ps.append(p.astype(BF16))
            alphas.append(alpha)
        pvs = [jnp.dot(vt_ref[b, 0, j], ps[b], preferred_element_type=F32) for b in range(nbatch)]
        for b in range(nbatch):
            acc_ref[b] = alphas[b] * acc_ref[b] + pvs[b]
        return tuple(new_m), tuple(new_l)

    split = jnp.maximum(i - (N_NEAR - 1), 0)
    carry = lax.fori_loop(0, split, functools.partial(step, near=False), (tuple(ms), tuple(ls)))
    ms, ls = lax.fori_loop(split, i, functools.partial(step, near=True), carry)
    for b in range(nbatch):
        o_ref[b] = (acc_ref[b] / ls[b]).T.astype(o_ref.dtype)


def _attention(rel_bias, qt, kn, vt, mask, bias, batch, seq):
    nb = seq // MOBA_BLOCK
    bg = max(c for c in range(1, ATT_BATCH_GROUP + 1) if batch % c == 0)
    once = pl.Buffered(1)
    out = pl.pallas_call(
        _attention_kernel,
        grid=(batch // bg, N_HEADS, nb),
        in_specs=[pl.BlockSpec(memory_space=pltpu.SMEM),
                  pl.BlockSpec((bg, 1, 1, HEAD_DIM, MOBA_BLOCK), lambda g, h, i: (g, h, i, 0, 0)),
                  pl.BlockSpec((bg, seq, HEAD_DIM), lambda g, h, i: (g, 0, h), pipeline_mode=once),
                  pl.BlockSpec((bg, 1, nb, HEAD_DIM, MOBA_BLOCK), lambda g, h, i: (g, h, 0, 0, 0), pipeline_mode=once),
                  pl.BlockSpec((bg, 1, 1, nb, MOBA_BLOCK), lambda g, h, i: (g, h, i, 0, 0)),
                  pl.BlockSpec((1, N_NEAR, MOBA_BLOCK, MOBA_BLOCK), lambda g, h, i: (h, 0, 0, 0))],
        out_specs=pl.BlockSpec((bg, MOBA_BLOCK, HEAD_DIM), lambda g, h, i: (g, i, h)),
        out_shape=jax.ShapeDtypeStruct((batch, seq, D_MODEL), BF16),
        scratch_shapes=[pltpu.VMEM((bg, HEAD_DIM, MOBA_BLOCK), F32)],
        compiler_params=_params("parallel", "parallel", "arbitrary"),
        name="moba_attention",
    )(rel_bias, qt, kn.reshape(batch, seq, D_MODEL), vt, mask, bias)
    return out.reshape(batch * seq, D_MODEL)


def kernel(x, mix_norm, ffn_norm, hg_w_in, hg_lb, hg_o_norm, hg_w_out, kv_norm, w_kv, k_norm, att_w_q, q_norm, att_w_o, rel_bias, moe_w_group, moe_b_group, moe_w_expert, moe_b_expert, moe_w_in, moe_w_out):
    batch, seq, d = x.shape
    assert d == D_MODEL and seq % GLA_TILE == 0 and seq % MOBA_BLOCK == 0
    assert mix_norm.shape[0] == 2 and hg_w_in.shape[0] == 1 and att_w_q.shape[0] == 1
    x0 = x.reshape(batch * seq, d)
    w_in_e = moe_w_in.astype(BF16)
    w_out_e = moe_w_out.astype(BF16)

    q, f, v, g = _hg_proj(x0, mix_norm[0:1], hg_lb, hg_w_in[0].astype(BF16))
    og = _gla(q, f, v, g, hg_o_norm[0:1], batch, seq)
    x1, hn, route = _proj_route(og, x0, hg_w_out[0].astype(BF16), ffn_norm[0:1],
                                *_router_weights(moe_w_group[0], moe_b_group[0], moe_w_expert[0], moe_b_expert[0]))
    x2 = _moe(x1, hn, route, w_in_e[0], w_out_e[0])

    nb = seq // MOBA_BLOCK
    kn, vt, qt, kmean = _kvq(x2, kv_norm[None, :], mix_norm[1:2], w_kv, att_w_q[0], k_norm[None, :], q_norm[0],
                             batch, nb)
    mask = _select(qt, kmean.reshape(batch, nb, d), batch, nb)
    att = _attention(rel_bias, qt, kn, vt, mask, _bias_tiles(rel_bias), batch, seq)
    x3, hn, route = _proj_route(att, x2, att_w_o[0].astype(BF16), ffn_norm[1:2],
                                *_router_weights(moe_w_group[1], moe_b_group[1], moe_w_expert[1], moe_b_expert[1]))
    out = _moe(x3, hn, route, w_in_e[1], w_out_e[1])
    return out.reshape(batch, seq, d)
```

```python
import functools
import math

import jax
import jax.numpy as jnp
import numpy as np
from jax import lax
from jax.experimental import pallas as pl
from jax.experimental.pallas import tpu as pltpu

F32 = jnp.float32
BF16 = jnp.bfloat16
HIGHEST = lax.Precision.HIGHEST

D_MODEL = 1024
N_HEADS = 8
HEAD_DIM = 128
HG_CHUNK = 64
ATT_SCALE = HEAD_DIM ** -0.5
MOBA_BLOCK = 256
MOBA_TOPK = 3
NUM_BUCKETS = 32
MAX_DISTANCE = 1024
N_GROUPS = 4
EXPERTS_PER_GROUP = 8
N_EXPERTS = N_GROUPS * EXPERTS_PER_GROUP
EXPERT_FF = D_MODEL // 2
EPS = 1e-6
NEG = -1e30
LOG2E = math.log2(math.e)
ATT_BATCH_GROUP = 4

LANES = 128
SUBLANES = 8
ROW_TILE = 256
GLA_TILE = 512
ROW_BLOCK = 256
VMEM_LIMIT = 56 * 1024 * 1024


def _near_block_count():
    max_exact = NUM_BUCKETS // 2
    delta = 1
    while True:
        dist = delta * MOBA_BLOCK - (MOBA_BLOCK - 1)
        steps = (math.log(max(dist, max_exact) / max_exact) / math.log(MAX_DISTANCE / max_exact)
                 * (NUM_BUCKETS - max_exact))
        if steps >= NUM_BUCKETS - 1 - max_exact + 0.5:
            return delta
        delta += 1


N_NEAR = _near_block_count()


def _params(*sem):
    return pltpu.CompilerParams(dimension_semantics=sem, vmem_limit_bytes=VMEM_LIMIT)


def _const_spec(shape):
    nd = len(shape)
    return pl.BlockSpec(shape, lambda *_: (0,) * nd, pipeline_mode=pl.Buffered(1))


def _rms(x, w):
    return x * lax.rsqrt(jnp.mean(x * x, axis=-1, keepdims=True) + EPS) * w


def _sigmoid(x):
    return 1.0 / (1.0 + jnp.exp(-x))


def _dot_nt(a, b):
    return lax.dot_general(a, b, (((1,), (1,)), ((), ())), preferred_element_type=F32)


def _head_norm(x, w):
    segs = [_rms(x[:, h * HEAD_DIM:(h + 1) * HEAD_DIM], w) for h in range(N_HEADS)]
    return jnp.concatenate(segs, axis=-1)


def _hg_proj_kernel(x_ref, nw_ref, lb_ref, w_ref, q_ref, f_ref, v_ref, g_ref):
    h = _rms(x_ref[...], nw_ref[...]).astype(BF16)
    lbl = lb_ref[...]
    e = jnp.exp(lbl - jnp.max(lbl, axis=0, keepdims=True))
    lb = e[0:1] / jnp.sum(e, axis=0, keepdims=True)
    d = D_MODEL
    q = jnp.dot(h, w_ref[:, 0:d], preferred_element_type=F32)
    q_ref[...] = q * (HEAD_DIM ** -0.5)
    fl = jnp.dot(h, w_ref[:, d:2 * d], preferred_element_type=F32)
    f_ref[...] = lb + (1.0 - lb) * _sigmoid(fl)
    v_ref[...] = jnp.dot(h, w_ref[:, 2 * d:3 * d], preferred_element_type=F32)
    g = jnp.dot(h, w_ref[:, 3 * d:4 * d], preferred_element_type=F32)
    g_ref[...] = g * _sigmoid(g)


def _hg_proj(x, nw, hg_lb, w_in):
    t = x.shape[0]
    out = jax.ShapeDtypeStruct((t, D_MODEL), F32)
    row = pl.BlockSpec((ROW_TILE, D_MODEL), lambda i: (i, 0))
    return pl.pallas_call(
        _hg_proj_kernel,
        grid=(t // ROW_TILE,),
        in_specs=[row, _const_spec((1, D_MODEL)), _const_spec(hg_lb.shape), _const_spec(w_in.shape)],
        out_specs=[row, row, row, row],
        out_shape=[out, out, out, out],
        compiler_params=_params("parallel"),
        name="hg_proj",
    )(x, nw, hg_lb, w_in)


def _gla_level_tables():
    c = HG_CHUNK
    t = np.arange(c)[:, None]
    s = np.arange(c)[None, :]
    cum, mask = [s <= t], [s == t]
    h = c // 2
    while h >= 1:
        mid = t // (2 * h) * (2 * h) + h
        cum.append(s <= mid - 1)
        mask.append((t // (2 * h) == s // (2 * h)) & (t % (2 * h) >= h) & (s % (2 * h) < h))
        h //= 2
    cum = np.concatenate(cum, 0).astype(np.float32)
    return np.tile(cum, (1, 3)), np.stack(mask).astype(np.float32)


def _gla_kernel(q_ref, f_ref, v_ref, g_ref, nw_ref, cum_ref, msk_ref, o_ref, st_ref):
    @pl.when(pl.program_id(1) == 0)
    def _():
        st_ref[...] = jnp.zeros_like(st_ref)

    c = HG_CHUNK
    levels = msk_ref.shape[0]
    nw = nw_ref[...]
    heads = [slice(h * HEAD_DIM, (h + 1) * HEAD_DIM) for h in range(N_HEADS)]

    def chunk(ci, carry):
        rs = pl.ds(pl.multiple_of(ci * c, c), c)
        cum = cum_ref[...]
        qs, ks, vs, bbs, scs = [], [], [], [], []
        for hs in heads:
            f = f_ref[rs, hs]
            lf = jnp.log(f) * LOG2E
            l1 = lf.astype(BF16)
            r1 = lf - l1.astype(F32)
            l2 = r1.astype(BF16)
            l3 = (r1 - l2.astype(F32)).astype(BF16)
            bbs.append(jnp.dot(cum, jnp.concatenate([l1, l2, l3], axis=0), preferred_element_type=F32))
            qs.append(q_ref[rs, hs])
            ks.append(1.0 - f)
            vs.append(v_ref[rs, hs].astype(BF16))
        for h in range(N_HEADS):
            q, k, bb = qs[h], ks[h], bbs[h]
            b = bb[0:c]
            a = _dot_nt(q.astype(BF16), k.astype(BF16)) * msk_ref[0]
            for lev in range(1, levels):
                e = jnp.exp2(-jnp.abs(b - bb[lev * c:(lev + 1) * c]))
                a = a + _dot_nt((q * e).astype(BF16), (k * e).astype(BF16)) * msk_ref[lev]
            scs.append(a.astype(BF16))
        outs = []
        for h in range(N_HEADS):
            q, k, v, b = qs[h], ks[h], vs[h], bbs[h][0:c]
            st = st_ref[h]
            o = jnp.dot(scs[h], v, preferred_element_type=F32)
            o = o + _dot_nt((q * jnp.exp2(b)).astype(BF16), st.astype(BF16))
            bl = b[c - 1:c]
            kd = (k * jnp.exp2(bl - b)).astype(BF16)
            upd = lax.dot_general(v, kd, (((0,), (0,)), ((), ())), preferred_element_type=F32)
            st_ref[h] = st * jnp.exp2(bl) + upd
            outs.append(o)
        for h, hs in enumerate(heads):
            o_ref[rs, hs] = (_rms(outs[h], nw) * g_ref[rs, hs]).astype(o_ref.dtype)
        return carry

    lax.fori_loop(0, GLA_TILE // c, chunk, 0)


def _gla(q, f, v, g, nw, batch, seq):
    t = q.shape[0]
    nl = seq // GLA_TILE
    cum, msk = _gla_level_tables()
    blk = pl.BlockSpec((GLA_TILE, D_MODEL), lambda b, l: (b * nl + l, 0))
    return pl.pallas_call(
        _gla_kernel,
        grid=(batch, nl),
        in_specs=[blk, blk, blk, blk, _const_spec((1, HEAD_DIM)), _const_spec(cum.shape), _const_spec(msk.shape)],
        out_specs=blk,
        out_shape=jax.ShapeDtypeStruct((t, D_MODEL), BF16),
        scratch_shapes=[pltpu.VMEM((N_HEADS, HEAD_DIM, HEAD_DIM), F32)],
        compiler_params=_params("parallel", "arbitrary"),
        name="gla",
    )(q, f, v, g, nw, jnp.asarray(cum, BF16), jnp.asarray(msk, F32))


def _proj_route_kernel(a_ref, x_ref, w_ref, nw_ref, wrh_ref, wrl_ref, br_ref, xo_ref, hn_ref, rt_ref, cnt_ref):
    y = x_ref[...] + jnp.dot(a_ref[...], w_ref[...], preferred_element_type=F32)
    xo_ref[...] = y
    hn = _rms(y, nw_ref[...])
    hn_ref[...] = hn
    hh = hn.astype(BF16)
    hl = (hn - hh.astype(F32)).astype(BF16)
    logits = (jnp.dot(hh, wrh_ref[...], preferred_element_type=F32)
              + jnp.dot(hl, wrh_ref[...], preferred_element_type=F32)
              + jnp.dot(hh, wrl_ref[...], preferred_element_type=F32)) + br_ref[...]
    lane = lax.broadcasted_iota(jnp.int32, logits.shape, 1)
    lanef = lane.astype(F32)
    big = float(LANES)
    gl = jnp.where(lane < N_GROUPS, logits, NEG)
    gmax = jnp.max(gl, axis=-1, keepdims=True)
    gidx = jnp.min(jnp.where(gl == gmax, lanef, big), axis=-1, keepdims=True)
    g_w = 1.0 / jnp.sum(jnp.exp(gl - gmax), axis=-1, keepdims=True)
    lo = N_GROUPS + gidx * EXPERTS_PER_GROUP
    el = jnp.where((lanef >= lo) & (lanef < lo + EXPERTS_PER_GROUP), logits, NEG)
    m1 = jnp.max(el, axis=-1, keepdims=True)
    i1 = jnp.min(jnp.where(el == m1, lanef, big), axis=-1, keepdims=True)
    el2 = jnp.where(lanef == i1, NEG, el)
    m2 = jnp.max(el2, axis=-1, keepdims=True)
    i2 = jnp.min(jnp.where(el2 == m2, lanef, big), axis=-1, keepdims=True)
    e2 = jnp.exp(m2 - m1)
    w1 = g_w / (1.0 + e2)
    w2 = g_w * e2 / (1.0 + e2)
    ex1, ex2 = i1 - N_GROUPS, i2 - N_GROUPS
    oh1 = jnp.where(lanef == ex1, 1.0, 0.0)
    oh2 = jnp.where(lanef == ex2, 1.0, 0.0)
    n = logits.shape[0]
    before = jnp.where(lax.broadcasted_iota(jnp.int32, (n, n), 0) > lax.broadcasted_iota(jnp.int32, (n, n), 1),
                       1.0, 0.0).astype(BF16)
    c1 = jnp.dot(before, oh1.astype(BF16), preferred_element_type=F32)
    c2 = jnp.dot(before, oh2.astype(BF16), preferred_element_type=F32)
    tot1 = jnp.sum(oh1, axis=0, keepdims=True)
    r1 = jnp.sum(oh1 * c1, axis=-1, keepdims=True)
    r2 = jnp.sum(oh2 * (c2 + tot1), axis=-1, keepdims=True)
    cnt_ref[0] = tot1 + jnp.sum(oh2, axis=0, keepdims=True)
    rt = jnp.where(lane == 0, ex1, 0.0)
    rt = jnp.where(lane == 1, ex2, rt)
    rt = jnp.where(lane == 2, w1, rt)
    rt = jnp.where(lane == 3, w2, rt)
    rt = jnp.where(lane == 4, r1, rt)
    rt_ref[...] = jnp.where(lane == 5, r2, rt)


def _proj_route(a, x, w, nw, wr_hi, wr_lo, br):
    t = x.shape[0]
    row = pl.BlockSpec((ROW_TILE, D_MODEL), lambda i: (i, 0))
    rt = pl.BlockSpec((ROW_TILE, LANES), lambda i: (i, 0))
    act = jax.ShapeDtypeStruct((t, D_MODEL), F32)
    return pl.pallas_call(
        _proj_route_kernel,
        grid=(t // ROW_TILE,),
        in_specs=[row, row, _const_spec(w.shape), _const_spec((1, D_MODEL)),
                  _const_spec(wr_hi.shape), _const_spec(wr_lo.shape), _const_spec((1, LANES))],
        out_specs=[row, row, rt, pl.BlockSpec((1, 1, LANES), lambda i: (i, 0, 0))],
        out_shape=[act, act, jax.ShapeDtypeStruct((t, LANES), F32),
                   jax.ShapeDtypeStruct((t // ROW_TILE, 1, LANES), F32)],
        compiler_params=_params("parallel"),
        name="proj_route",
    )(a, x, w, nw, wr_hi, wr_lo, br)


def _router_weights(w_group, b_group, w_expert, b_expert):
    pad = LANES - N_GROUPS - N_EXPERTS
    w = jnp.concatenate([w_group, w_expert, jnp.zeros((D_MODEL, pad), F32)], axis=1)
    b = jnp.concatenate([b_group, b_expert, jnp.zeros((pad,), F32)])[None, :]
    hi = w.astype(BF16)
    lo = (w - hi.astype(F32)).astype(BF16)
    return hi, lo, b


def _dispatch_kernel(dest_ref, seg_ref, nused_ref, hn_ref, xbuf_hbm, zero_ref, sem, zsem):
    def zero_row(r):
        return pltpu.make_async_copy(zero_ref.at[pl.ds(0, 1)], xbuf_hbm.at[pl.ds(r, 1)], zsem)

    def zero_block(b):
        return pltpu.make_async_copy(zero_ref, xbuf_hbm.at[pl.ds(pl.multiple_of(b * ROW_BLOCK, ROW_BLOCK), ROW_BLOCK)], zsem)

    def for_each_padding_row(fn):
        def per_expert(e, carry):
            def row(r, c):
                fn(r)
                return c
            return lax.fori_loop(seg_ref[0, e], seg_ref[1, e], row, carry)
        lax.fori_loop(0, N_EXPERTS, per_expert, 0)

    def for_each_unused_block(fn):
        def block(b, c):
            fn(b)
            return c
        lax.fori_loop(nused_ref[0], xbuf_hbm.shape[0] // ROW_BLOCK, block, 0)

    @pl.when(pl.program_id(0) == 0)
    def _():
        zero_ref[...] = jnp.zeros_like(zero_ref)
        for_each_padding_row(lambda r: zero_row(r).start())
        for_each_unused_block(lambda b: zero_block(b).start())
        for_each_padding_row(lambda r: zero_row(r).wait())
        for_each_unused_block(lambda b: zero_block(b).wait())

    def issue(r, carry):
        for k in range(2):
            pltpu.make_async_copy(hn_ref.at[pl.ds(r, 1)], xbuf_hbm.at[pl.ds(dest_ref[0, 0, 2 * r + k], 1)], sem).start()
        return carry

    lax.fori_loop(0, ROW_TILE, issue, 0)
    for k in range(2):
        pltpu.make_async_copy(hn_ref, xbuf_hbm.at[pl.ds(0, ROW_TILE)], sem).wait()


def _dispatch(dest3, seg, nused, hn, cap):
    t = hn.shape[0]
    return pl.pallas_call(
        _dispatch_kernel,
        grid=(t // ROW_TILE,),
        in_specs=[pl.BlockSpec((1, 1, 2 * ROW_TILE), lambda i: (i, 0, 0), memory_space=pltpu.SMEM),
                  pl.BlockSpec(memory_space=pltpu.SMEM), pl.BlockSpec(memory_space=pltpu.SMEM),
                  pl.BlockSpec((ROW_TILE, D_MODEL), lambda i: (i, 0))],
        out_specs=pl.BlockSpec(memory_space=pl.ANY),
        out_shape=jax.ShapeDtypeStruct((cap, D_MODEL), F32),
        scratch_shapes=[pltpu.VMEM((ROW_BLOCK, D_MODEL), F32), pltpu.SemaphoreType.DMA, pltpu.SemaphoreType.DMA],
        compiler_params=pltpu.CompilerParams(dimension_semantics=("arbitrary",), has_side_effects=True),
        name="moe_dispatch",
    )(dest3, seg, nused, hn)


def _experts_kernel(be_ref, nused_ref, x_ref, wi_ref, wo_ref, y_ref, wi_bf, wo_bf):
    i = pl.program_id(0)
    used = i < nused_ref[0]
    new_expert = (i == 0) | (be_ref[i] != be_ref[jnp.maximum(i - 1, 0)])

    @pl.when(used & new_expert)
    def _():
        wi_bf[...] = wi_ref[0, 0].astype(BF16)
        wo_bf[...] = wo_ref[0, 0].astype(BF16)

    @pl.when(used)
    def _():
        hu = jnp.dot(x_ref[...].astype(BF16), wi_bf[...], preferred_element_type=F32)
        u = hu[:, :EXPERT_FF]
        act = u * _sigmoid(u) * hu[:, EXPERT_FF:]
        y_ref[...] = jnp.dot(act.astype(BF16), wo_bf[...], preferred_element_type=F32)

    @pl.when(jnp.logical_not(used))
    def _():
        y_ref[...] = jnp.zeros_like(y_ref)


def _experts(block_expert, nused, xbuf, w_in, w_out, layer):
    cap = xbuf.shape[0]
    grid_spec = pltpu.PrefetchScalarGridSpec(
        num_scalar_prefetch=2,
        grid=(cap // ROW_BLOCK,),
        in_specs=[pl.BlockSpec((ROW_BLOCK, D_MODEL), lambda i, be, nu: (jnp.minimum(i, nu[0] - 1), 0)),
                  pl.BlockSpec((1, 1, D_MODEL, 2 * EXPERT_FF), lambda i, be, nu: (layer, be[i], 0, 0)),
                  pl.BlockSpec((1, 1, EXPERT_FF, D_MODEL), lambda i, be, nu: (layer, be[i], 0, 0))],
        out_specs=pl.BlockSpec((ROW_BLOCK, D_MODEL), lambda i, be, nu: (i, 0)),
        scratch_shapes=[pltpu.VMEM((D_MODEL, 2 * EXPERT_FF), BF16), pltpu.VMEM((EXPERT_FF, D_MODEL), BF16)],
    )
    return pl.pallas_call(
        _experts_kernel,
        grid_spec=grid_spec,
        out_shape=jax.ShapeDtypeStruct((cap, D_MODEL), F32),
        compiler_params=_params("arbitrary"),
        name="moe_experts",
    )(block_expert, nused, xbuf, w_in, w_out)


def _combine_kernel(dest_ref, rt_ref, x_ref, y_hbm, o_ref, buf, sem):
    def issue(r, carry):
        for k in range(2):
            pltpu.make_async_copy(y_hbm.at[pl.ds(dest_ref[0, 0, 2 * r + k], 1)], buf.at[k, pl.ds(r, 1)], sem).start()
        return carry

    lax.fori_loop(0, ROW_TILE, issue, 0)
    for k in range(2):
        pltpu.make_async_copy(y_hbm.at[pl.ds(0, ROW_TILE)], buf.at[k], sem).wait()
    rt = rt_ref[...]
    o_ref[...] = x_ref[...] + rt[:, 2:3] * buf[0] + rt[:, 3:4] * buf[1]


def _combine(dest3, route, x, ybuf):
    t = x.shape[0]
    row = pl.BlockSpec((ROW_TILE, D_MODEL), lambda i: (i, 0))
    return pl.pallas_call(
        _combine_kernel,
        grid=(t // ROW_TILE,),
        in_specs=[pl.BlockSpec((1, 1, 2 * ROW_TILE), lambda i: (i, 0, 0), memory_space=pltpu.SMEM),
                  pl.BlockSpec((ROW_TILE, LANES), lambda i: (i, 0)), row,
                  pl.BlockSpec(memory_space=pl.ANY)],
        out_specs=row,
        out_shape=jax.ShapeDtypeStruct((t, D_MODEL), F32),
        scratch_shapes=[pltpu.VMEM((2, ROW_TILE, D_MODEL), F32), pltpu.SemaphoreType.DMA],
        compiler_params=_params("arbitrary"),
        name="moe_combine",
    )(dest3, route, x, ybuf)


def _moe(x, hn, route, counts, w_in, w_out, layer):
    t = x.shape[0]
    n_rb = -(-2 * t // ROW_BLOCK) + N_EXPERTS
    cap = n_rb * ROW_BLOCK
    cnt = counts[:, 0, :N_EXPERTS].astype(jnp.int32)
    total = jnp.sum(cnt, axis=0)
    padded = (total + ROW_BLOCK - 1) // ROW_BLOCK * ROW_BLOCK
    pad_end = jnp.cumsum(padded)
    pad_start = pad_end - padded
    tile_start = pad_start[None, :] + jnp.cumsum(cnt, axis=0) - cnt
    ids = route[:, 0:2].astype(jnp.int32)
    picked = ids[:, :, None] == jnp.arange(N_EXPERTS, dtype=jnp.int32)[None, None, :]
    starts = jnp.repeat(tile_start, ROW_TILE, axis=0)[:, None, :]
    dest = jnp.sum(jnp.where(picked, starts, 0), axis=-1) + route[:, 4:6].astype(jnp.int32)
    dest3 = dest.reshape(t // ROW_TILE, 1, 2 * ROW_TILE)
    seg = jnp.stack([pad_start + total, pad_end]).astype(jnp.int32)
    block_expert = jnp.minimum(
        jnp.searchsorted(pad_end, jnp.arange(n_rb, dtype=jnp.int32) * ROW_BLOCK, side="right"),
        N_EXPERTS - 1).astype(jnp.int32)
    nused = (pad_end[-1:] // ROW_BLOCK).astype(jnp.int32)
    xbuf = _dispatch(dest3, seg, nused, hn, cap)
    ybuf = _experts(block_expert, nused, xbuf, w_in, w_out, layer)
    return _combine(dest3, route, x, ybuf)


def _kvq_kernel(x_ref, kvn_ref, qn_ref, wk_ref, wvt_ref, wqt_ref, knw_ref, qnw_ref, k_ref, vt_ref, qt_ref, km_ref):
    x = x_ref[...]
    hkv = _rms(x, kvn_ref[...]).astype(BF16)
    k = _head_norm(jnp.dot(hkv, wk_ref[...], preferred_element_type=F32), knw_ref[...])
    k_ref[...] = k.astype(k_ref.dtype)
    km_ref[0] = jnp.mean(k, axis=0, keepdims=True)
    vt = _dot_nt(wvt_ref[...], hkv)
    hq = _rms(x, qn_ref[...]).astype(BF16)
    qt = _dot_nt(wqt_ref[...], hq)
    qnw = qnw_ref[...]
    for h in range(N_HEADS):
        hs = slice(h * HEAD_DIM, (h + 1) * HEAD_DIM)
        vt_ref[0, h, 0] = vt[hs].astype(vt_ref.dtype)
        seg = qt[hs]
        qt_ref[0, h, 0] = seg * lax.rsqrt(jnp.mean(seg * seg, axis=0, keepdims=True) + EPS) * qnw


def _kvq(x, kv_norm, q_prenorm, w_kv, w_q, k_norm, q_norm, batch, nb):
    t = x.shape[0]
    row = pl.BlockSpec((MOBA_BLOCK, D_MODEL), lambda i: (i, 0))
    tblk = pl.BlockSpec((1, N_HEADS, 1, HEAD_DIM, MOBA_BLOCK), lambda i: (i // nb, 0, i % nb, 0, 0))
    tshape = (batch, N_HEADS, nb, HEAD_DIM, MOBA_BLOCK)
    w_k = w_kv[:, :D_MODEL].astype(BF16)
    w_vt = w_kv[:, D_MODEL:].T.astype(BF16)
    w_qt = w_q.T.astype(BF16)
    qnw = jnp.broadcast_to(q_norm.reshape(HEAD_DIM, 1), (HEAD_DIM, MOBA_BLOCK))
    wspec = _const_spec((D_MODEL, D_MODEL))
    return pl.pallas_call(
        _kvq_kernel,
        grid=(t // MOBA_BLOCK,),
        in_specs=[row, _const_spec((1, D_MODEL)), _const_spec((1, D_MODEL)), wspec, wspec, wspec,
                  _const_spec((1, HEAD_DIM)), _const_spec((HEAD_DIM, MOBA_BLOCK))],
        out_specs=[row, tblk, tblk, pl.BlockSpec((1, 1, D_MODEL), lambda i: (i, 0, 0))],
        out_shape=[jax.ShapeDtypeStruct((t, D_MODEL), BF16), jax.ShapeDtypeStruct(tshape, BF16),
                   jax.ShapeDtypeStruct(tshape, F32),
                   jax.ShapeDtypeStruct((t // MOBA_BLOCK, 1, D_MODEL), F32)],
        compiler_params=_params("parallel"),
        name="kvq",
    )(x, kv_norm, q_prenorm, w_k, w_vt, w_qt, k_norm, qnw)


def _bias_tiles_kernel(rb_ref, o_ref):
    h, delta = pl.program_id(0), pl.program_id(1)
    shape = (MOBA_BLOCK, MOBA_BLOCK)
    dist = delta * MOBA_BLOCK + lax.broadcasted_iota(jnp.int32, shape, 1) - lax.broadcasted_iota(jnp.int32, shape, 0)
    n = jnp.maximum(dist, 0)
    max_exact = NUM_BUCKETS // 2
    log_ratio = (jnp.log(jnp.maximum(n, max_exact).astype(F32) / max_exact)
                 / math.log(MAX_DISTANCE / max_exact))
    large = max_exact + (log_ratio * (NUM_BUCKETS - max_exact)).astype(jnp.int32)
    bucket = jnp.where(n < max_exact, n, jnp.minimum(large, NUM_BUCKETS - 1))
    bias = jnp.zeros(shape, F32)
    for b in range(NUM_BUCKETS):
        bias = jnp.where(bucket == b, rb_ref[b, h], bias)
    o_ref[0, 0] = jnp.where(dist >= 0, bias * LOG2E, NEG)


def _bias_tiles(rel_bias):
    return pl.pallas_call(
        _bias_tiles_kernel,
        grid=(N_HEADS, N_NEAR + 1),
        in_specs=[pl.BlockSpec(memory_space=pltpu.SMEM)],
        out_specs=pl.BlockSpec((1, 1, MOBA_BLOCK, MOBA_BLOCK), lambda h, d: (h, d, 0, 0)),
        out_shape=jax.ShapeDtypeStruct((N_HEADS, N_NEAR + 1, MOBA_BLOCK, MOBA_BLOCK), F32),
        compiler_params=_params("parallel", "parallel"),
        name="bias_tiles",
    )(rel_bias)


def _select_kernel(qt_ref, km_ref, o_ref):
    i = pl.program_id(1)
    nb = km_ref.shape[1]
    km = km_ref[0]
    blk = lax.broadcasted_iota(jnp.int32, (nb, MOBA_BLOCK), 0)
    blkf = blk.astype(F32)
    for h in range(N_HEADS):
        gate = jnp.dot(km[:, h * HEAD_DIM:(h + 1) * HEAD_DIM], qt_ref[0, h, 0],
                       precision=HIGHEST, preferred_element_type=F32)
        g = jnp.where(blk < i, gate, NEG)
        mask = jnp.full((nb, MOBA_BLOCK), NEG, F32)
        for _ in range(MOBA_TOPK):
            m = jnp.max(g, axis=0, keepdims=True)
            idx = jnp.min(jnp.where(g == m, blkf, float(nb)), axis=0, keepdims=True)
            hit = (blkf == idx) & (m > 0.5 * NEG)
            mask = jnp.where(hit, 0.0, mask)
            g = jnp.where(blkf == idx, NEG, g)
        o_ref[0, h, 0] = jnp.where(blk == i, 0.0, mask)


def _select(qt, kmean, batch, nb):
    tblk = pl.BlockSpec((1, N_HEADS, 1, HEAD_DIM, MOBA_BLOCK), lambda b, i: (b, 0, i, 0, 0))
    return pl.pallas_call(
        _select_kernel,
        grid=(batch, nb),
        in_specs=[tblk, pl.BlockSpec((1, nb, D_MODEL), lambda b, i: (b, 0, 0))],
        out_specs=pl.BlockSpec((1, N_HEADS, 1, nb, MOBA_BLOCK), lambda b, i: (b, 0, i, 0, 0)),
        out_shape=jax.ShapeDtypeStruct((batch, N_HEADS, nb, nb, MOBA_BLOCK), F32),
        compiler_params=_params("parallel", "parallel"),
        name="moba_select",
    )(qt, kmean)


def _attention_kernel(qt_ref, k_ref, vt_ref, mask_ref, bias_ref, o_ref, acc_ref, ss_ref, ps_ref):
    i = pl.program_id(2)
    blk = MOBA_BLOCK
    nbatch = qt_ref.shape[0]
    last = i
    qts = [(qt_ref[b, 0, 0] * (ATT_SCALE * LOG2E)).astype(BF16) for b in range(nbatch)]

    def block_of(t):
        return jnp.where((t == 0) | (t > last), i, t - 1)

    def stage_s(t, slot):
        j = block_of(t)
        for b in range(nbatch):
            ss_ref[slot, b] = jnp.dot(k_ref[b, pl.ds(pl.multiple_of(j * blk, blk), blk), :], qts[b],
                                      preferred_element_type=F32)

    def stage_f(t, slot, ms, ls):
        j = block_of(t)
        bias = bias_ref[0, jnp.minimum(i - j, N_NEAR)]
        dead = jnp.where(t > last, NEG, 0.0)
        new_m, new_l, alphas = [], [], []
        for b in range(nbatch):
            s = ss_ref[slot, b] + bias
            rowb = mask_ref[b, 0, 0, pl.ds(j, 1), :] + dead
            mc = jnp.maximum(ms[b], jnp.max(s, axis=0, keepdims=True) + rowb)
            p = jnp.exp2(s - (mc - rowb))
            alpha = jnp.exp2(ms[b] - mc)
            ps_ref[slot, b] = p.astype(BF16)
            new_m.append(mc)
            new_l.append(alpha * ls[b] + jnp.sum(p, axis=0, keepdims=True))
            alphas.append(alpha)
        return tuple(new_m), tuple(new_l), tuple(alphas)

    def stage_a(t, slot, alphas):
        j = block_of(t)
        pvs = [jnp.dot(vt_ref[b, 0, j], ps_ref[slot, b], preferred_element_type=F32) for b in range(nbatch)]
        for b in range(nbatch):
            acc_ref[b] = alphas[b] * acc_ref[b] + pvs[b]

    stage_s(0, 0)
    ps_ref[1] = jnp.zeros(ps_ref.shape[1:], BF16)
    acc_ref[...] = jnp.zeros_like(acc_ref)
    row = (1, blk)
    init = (tuple(jnp.full(row, NEG, F32) for _ in range(nbatch)), tuple(jnp.zeros(row, F32) for _ in range(nbatch)),
            tuple(jnp.ones(row, F32) for _ in range(nbatch)))

    def two_steps(tt, carry):
        ms, ls, alphas = carry
        t = 2 * tt
        stage_s(t + 1, 1)
        ms, ls, alphas0 = stage_f(t, 0, ms, ls)
        stage_a(jnp.maximum(t - 1, 0), 1, alphas)
        stage_s(t + 2, 0)
        ms, ls, alphas1 = stage_f(t + 1, 1, ms, ls)
        stage_a(t, 0, alphas0)
        return ms, ls, alphas1

    trips = (last + 2) // 2
    ms, ls, alphas = lax.fori_loop(0, trips, two_steps, init)
    stage_a(2 * trips - 1, 1, alphas)
    for b in range(nbatch):
        o_ref[b] = (acc_ref[b] / ls[b]).T.astype(o_ref.dtype)


def _attention(qt, kn, vt, mask, bias, batch, seq):
    nb = seq // MOBA_BLOCK
    bg = max(c for c in range(1, ATT_BATCH_GROUP + 1) if batch % c == 0)
    once = pl.Buffered(1)
    tile = (2, bg, MOBA_BLOCK, MOBA_BLOCK)
    out = pl.pallas_call(
        _attention_kernel,
        grid=(batch // bg, N_HEADS, nb),
        in_specs=[pl.BlockSpec((bg, 1, 1, HEAD_DIM, MOBA_BLOCK), lambda g, h, i: (g, h, i, 0, 0)),
                  pl.BlockSpec((bg, seq, HEAD_DIM), lambda g, h, i: (g, 0, h), pipeline_mode=once),
                  pl.BlockSpec((bg, 1, nb, HEAD_DIM, MOBA_BLOCK), lambda g, h, i: (g, h, 0, 0, 0), pipeline_mode=once),
                  pl.BlockSpec((bg, 1, 1, nb, MOBA_BLOCK), lambda g, h, i: (g, h, i, 0, 0)),
                  pl.BlockSpec((1, N_NEAR + 1, MOBA_BLOCK, MOBA_BLOCK), lambda g, h, i: (h, 0, 0, 0))],
        out_specs=pl.BlockSpec((bg, MOBA_BLOCK, HEAD_DIM), lambda g, h, i: (g, i, h)),
        out_shape=jax.ShapeDtypeStruct((batch, seq, D_MODEL), BF16),
        scratch_shapes=[pltpu.VMEM((bg, HEAD_DIM, MOBA_BLOCK), F32), pltpu.VMEM(tile, F32), pltpu.VMEM(tile, BF16)],
        compiler_params=_params("parallel", "parallel", "arbitrary"),
        name="moba_attention",
    )(qt, kn.reshape(batch, seq, D_MODEL), vt, mask, bias)
    return out.reshape(batch * seq, D_MODEL)


def kernel(x, mix_norm, ffn_norm, hg_w_in, hg_lb, hg_o_norm, hg_w_out, kv_norm, w_kv, k_norm, att_w_q, q_norm, att_w_o, rel_bias, moe_w_group, moe_b_group, moe_w_expert, moe_b_expert, moe_w_in, moe_w_out):
    batch, seq, d = x.shape
    assert d == D_MODEL and seq % GLA_TILE == 0 and seq % MOBA_BLOCK == 0
    assert mix_norm.shape[0] == 2 and hg_w_in.shape[0] == 1 and att_w_q.shape[0] == 1
    x0 = x.reshape(batch * seq, d)

    q, f, v, g = _hg_proj(x0, mix_norm[0:1], hg_lb, hg_w_in[0].astype(BF16))
    og = _gla(q, f, v, g, hg_o_norm[0:1], batch, seq)
    x1, hn, route, counts = _proj_route(
        og, x0, hg_w_out[0].astype(BF16), ffn_norm[0:1],
        *_router_weights(moe_w_group[0], moe_b_group[0], moe_w_expert[0], moe_b_expert[0]))
    x2 = _moe(x1, hn, route, counts, moe_w_in, moe_w_out, 0)

    nb = seq // MOBA_BLOCK
    kn, vt, qt, kmean = _kvq(x2, kv_norm[None, :], mix_norm[1:2], w_kv, att_w_q[0], k_norm[None, :], q_norm[0],
                             batch, nb)
    mask = _select(qt, kmean.reshape(batch, nb, d), batch, nb)
    att = _attention(qt, kn, vt, mask, _bias_tiles(rel_bias), batch, seq)
    x3, hn, route, counts = _proj_route(
        att, x2, att_w_o[0].astype(BF16), ffn_norm[1:2],
        *_router_weights(moe_w_group[1], moe_b_group[1], moe_w_expert[1], moe_b_expert[1]))
    out = _moe(x3, hn, route, counts, moe_w_in, moe_w_out, 1)
    return out.reshape(batch, seq, d)
```

```python
import functools
import math

import jax
import jax.numpy as jnp
import numpy as np
from jax import lax
from jax.experimental import pallas as pl
from jax.experimental.pallas import tpu as pltpu

F32 = jnp.float32
BF16 = jnp.bfloat16
HIGHEST = lax.Precision.HIGHEST

D_MODEL = 1024
N_HEADS = 8
HEAD_DIM = 128
HG_CHUNK = 64
ATT_SCALE = HEAD_DIM ** -0.5
MOBA_BLOCK = 256
MOBA_TOPK = 3
NUM_BUCKETS = 32
MAX_DISTANCE = 1024
N_GROUPS = 4
EXPERTS_PER_GROUP = 8
N_EXPERTS = N_GROUPS * EXPERTS_PER_GROUP
EXPERT_FF = D_MODEL // 2
EPS = 1e-6
NEG = -1e30
LOG2E = math.log2(math.e)
ATT_BATCH_GROUP = 4

LANES = 128
SUBLANES = 8
ROW_TILE = 256
GLA_TILE = 512
ROW_BLOCK = 256
ISSUE_UNROLL = 8
VMEM_LIMIT = 56 * 1024 * 1024


def _near_block_count():
    max_exact = NUM_BUCKETS // 2
    delta = 1
    while True:
        dist = delta * MOBA_BLOCK - (MOBA_BLOCK - 1)
        steps = (math.log(max(dist, max_exact) / max_exact) / math.log(MAX_DISTANCE / max_exact)
                 * (NUM_BUCKETS - max_exact))
        if steps >= NUM_BUCKETS - 1 - max_exact + 0.5:
            return delta
        delta += 1


N_NEAR = _near_block_count()
ATT_NEAR_STEPS = N_NEAR + 1 + (N_NEAR + 1) % 2
V_ROWS = HEAD_DIM + 16


def _params(*sem):
    return pltpu.CompilerParams(dimension_semantics=sem, vmem_limit_bytes=VMEM_LIMIT)


def _const_spec(shape):
    nd = len(shape)
    return pl.BlockSpec(shape, lambda *_: (0,) * nd, pipeline_mode=pl.Buffered(1))


def _rms(x, w):
    return x * lax.rsqrt(jnp.mean(x * x, axis=-1, keepdims=True) + EPS) * w


def _sigmoid(x):
    return 1.0 / (1.0 + jnp.exp(-x))


def _dot_nt(a, b):
    return lax.dot_general(a, b, (((1,), (1,)), ((), ())), preferred_element_type=F32)


def _head_norm(x, w):
    segs = [_rms(x[:, h * HEAD_DIM:(h + 1) * HEAD_DIM], w) for h in range(N_HEADS)]
    return jnp.concatenate(segs, axis=-1)


def _hg_proj_kernel(x_ref, nw_ref, lb_ref, w_ref, q_ref, f_ref, v_ref, g_ref):
    h = _rms(x_ref[...], nw_ref[...]).astype(BF16)
    lbl = lb_ref[...]
    e = jnp.exp(lbl - jnp.max(lbl, axis=0, keepdims=True))
    lb = e[0:1] / jnp.sum(e, axis=0, keepdims=True)
    d = D_MODEL
    q = jnp.dot(h, w_ref[:, 0:d], preferred_element_type=F32)
    q_ref[...] = q * (HEAD_DIM ** -0.5)
    fl = jnp.dot(h, w_ref[:, d:2 * d], preferred_element_type=F32)
    f_ref[...] = lb + (1.0 - lb) * _sigmoid(fl)
    v_ref[...] = jnp.dot(h, w_ref[:, 2 * d:3 * d], preferred_element_type=F32)
    g = jnp.dot(h, w_ref[:, 3 * d:4 * d], preferred_element_type=F32)
    g_ref[...] = g * _sigmoid(g)


def _hg_proj(x, nw, hg_lb, w_in):
    t = x.shape[0]
    out = jax.ShapeDtypeStruct((t, D_MODEL), F32)
    row = pl.BlockSpec((ROW_TILE, D_MODEL), lambda i: (i, 0))
    return pl.pallas_call(
        _hg_proj_kernel,
        grid=(t // ROW_TILE,),
        in_specs=[row, _const_spec((1, D_MODEL)), _const_spec(hg_lb.shape), _const_spec(w_in.shape)],
        out_specs=[row, row, row, row],
        out_shape=[out, out, out, out],
        compiler_params=_params("parallel"),
        name="hg_proj",
    )(x, nw, hg_lb, w_in)


def _gla_level_tables():
    c = HG_CHUNK
    t = np.arange(c)[:, None]
    s = np.arange(c)[None, :]
    cum, mask = [s <= t], [s == t]
    h = c // 2
    while h >= 1:
        mid = t // (2 * h) * (2 * h) + h
        cum.append(s <= mid - 1)
        mask.append((t // (2 * h) == s // (2 * h)) & (t % (2 * h) >= h) & (s % (2 * h) < h))
        h //= 2
    cum = np.concatenate(cum, 0).astype(np.float32)
    return np.tile(cum, (1, 3)), np.stack(mask).astype(np.float32)


def _gla_kernel(q_ref, f_ref, v_ref, g_ref, nw_ref, cum_ref, msk_ref, o_ref, st_ref):
    @pl.when(pl.program_id(1) == 0)
    def _():
        st_ref[...] = jnp.zeros_like(st_ref)

    c = HG_CHUNK
    levels = msk_ref.shape[0]
    nw = nw_ref[...]
    heads = [slice(h * HEAD_DIM, (h + 1) * HEAD_DIM) for h in range(N_HEADS)]

    def chunk(ci, carry):
        rs = pl.ds(pl.multiple_of(ci * c, c), c)
        cum = cum_ref[...]
        qs, ks, vs, bbs, scs = [], [], [], [], []
        for hs in heads:
            f = f_ref[rs, hs]
            lf = jnp.log(f) * LOG2E
            l1 = lf.astype(BF16)
            r1 = lf - l1.astype(F32)
            l2 = r1.astype(BF16)
            l3 = (r1 - l2.astype(F32)).astype(BF16)
            bbs.append(jnp.dot(cum, jnp.concatenate([l1, l2, l3], axis=0), preferred_element_type=F32))
            qs.append(q_ref[rs, hs])
            ks.append(1.0 - f)
            vs.append(v_ref[rs, hs].astype(BF16))
        for h in range(N_HEADS):
            q, k, bb = qs[h], ks[h], bbs[h]
            b = bb[0:c]
            a = _dot_nt(q.astype(BF16), k.astype(BF16)) * msk_ref[0]
            for lev in range(1, levels):
                e = jnp.exp2(-jnp.abs(b - bb[lev * c:(lev + 1) * c]))
                a = a + _dot_nt((q * e).astype(BF16), (k * e).astype(BF16)) * msk_ref[lev]
            scs.append(a.astype(BF16))
        outs = []
        for h in range(N_HEADS):
            q, k, v, b = qs[h], ks[h], vs[h], bbs[h][0:c]
            st = st_ref[h]
            o = jnp.dot(scs[h], v, preferred_element_type=F32)
            o = o + _dot_nt((q * jnp.exp2(b)).astype(BF16), st.astype(BF16))
            bl = b[c - 1:c]
            kd = (k * jnp.exp2(bl - b)).astype(BF16)
            upd = lax.dot_general(v, kd, (((0,), (0,)), ((), ())), preferred_element_type=F32)
            st_ref[h] = st * jnp.exp2(bl) + upd
            outs.append(o)
        for h, hs in enumerate(heads):
            o_ref[rs, hs] = (_rms(outs[h], nw) * g_ref[rs, hs]).astype(o_ref.dtype)
        return carry

    lax.fori_loop(0, GLA_TILE // c, chunk, 0)


def _gla(q, f, v, g, nw, batch, seq):
    t = q.shape[0]
    nl = seq // GLA_TILE
    cum, msk = _gla_level_tables()
    blk = pl.BlockSpec((GLA_TILE, D_MODEL), lambda b, l: (b * nl + l, 0))
    return pl.pallas_call(
        _gla_kernel,
        grid=(batch, nl),
        in_specs=[blk, blk, blk, blk, _const_spec((1, HEAD_DIM)), _const_spec(cum.shape), _const_spec(msk.shape)],
        out_specs=blk,
        out_shape=jax.ShapeDtypeStruct((t, D_MODEL), BF16),
        scratch_shapes=[pltpu.VMEM((N_HEADS, HEAD_DIM, HEAD_DIM), F32)],
        compiler_params=_params("parallel", "arbitrary"),
        name="gla",
    )(q, f, v, g, nw, jnp.asarray(cum, BF16), jnp.asarray(msk, F32))


def _proj_route_kernel(a_ref, x_ref, w_ref, nw_ref, wrh_ref, wrl_ref, br_ref, xo_ref, hn_ref, rt_ref, cnt_ref):
    y = x_ref[...] + jnp.dot(a_ref[...], w_ref[...], preferred_element_type=F32)
    xo_ref[...] = y
    hn = _rms(y, nw_ref[...])
    hn_ref[...] = hn
    hh = hn.astype(BF16)
    hl = (hn - hh.astype(F32)).astype(BF16)
    logits = (jnp.dot(hh, wrh_ref[...], preferred_element_type=F32)
              + jnp.dot(hl, wrh_ref[...], preferred_element_type=F32)
              + jnp.dot(hh, wrl_ref[...], preferred_element_type=F32)) + br_ref[...]
    lane = lax.broadcasted_iota(jnp.int32, logits.shape, 1)
    lanef = lane.astype(F32)
    big = float(LANES)
    gl = jnp.where(lane < N_GROUPS, logits, NEG)
    gmax = jnp.max(gl, axis=-1, keepdims=True)
    gidx = jnp.min(jnp.where(gl == gmax, lanef, big), axis=-1, keepdims=True)
    g_w = 1.0 / jnp.sum(jnp.exp(gl - gmax), axis=-1, keepdims=True)
    lo = N_GROUPS + gidx * EXPERTS_PER_GROUP
    el = jnp.where((lanef >= lo) & (lanef < lo + EXPERTS_PER_GROUP), logits, NEG)
    m1 = jnp.max(el, axis=-1, keepdims=True)
    i1 = jnp.min(jnp.where(el == m1, lanef, big), axis=-1, keepdims=True)
    el2 = jnp.where(lanef == i1, NEG, el)
    m2 = jnp.max(el2, axis=-1, keepdims=True)
    i2 = jnp.min(jnp.where(el2 == m2, lanef, big), axis=-1, keepdims=True)
    e2 = jnp.exp(m2 - m1)
    w1 = g_w / (1.0 + e2)
    w2 = g_w * e2 / (1.0 + e2)
    ex1, ex2 = i1 - N_GROUPS, i2 - N_GROUPS
    oh1 = jnp.where(lanef == ex1, 1.0, 0.0)
    oh2 = jnp.where(lanef == ex2, 1.0, 0.0)
    n = logits.shape[0]
    before = jnp.where(lax.broadcasted_iota(jnp.int32, (n, n), 0) > lax.broadcasted_iota(jnp.int32, (n, n), 1),
                       1.0, 0.0).astype(BF16)
    c1 = jnp.dot(before, oh1.astype(BF16), preferred_element_type=F32)
    c2 = jnp.dot(before, oh2.astype(BF16), preferred_element_type=F32)
    tot1 = jnp.sum(oh1, axis=0, keepdims=True)
    r1 = jnp.sum(oh1 * c1, axis=-1, keepdims=True)
    r2 = jnp.sum(oh2 * (c2 + tot1), axis=-1, keepdims=True)
    cnt_ref[0] = tot1 + jnp.sum(oh2, axis=0, keepdims=True)
    rt = jnp.where(lane == 0, ex1, 0.0)
    rt = jnp.where(lane == 1, ex2, rt)
    rt = jnp.where(lane == 2, w1, rt)
    rt = jnp.where(lane == 3, w2, rt)
    rt = jnp.where(lane == 4, r1, rt)
    rt_ref[...] = jnp.where(lane == 5, r2, rt)


def _proj_route(a, x, w, nw, wr_hi, wr_lo, br):
    t = x.shape[0]
    row = pl.BlockSpec((ROW_TILE, D_MODEL), lambda i: (i, 0))
    rt = pl.BlockSpec((ROW_TILE, LANES), lambda i: (i, 0))
    act = jax.ShapeDtypeStruct((t, D_MODEL), F32)
    return pl.pallas_call(
        _proj_route_kernel,
        grid=(t // ROW_TILE,),
        in_specs=[row, row, _const_spec(w.shape), _const_spec((1, D_MODEL)),
                  _const_spec(wr_hi.shape), _const_spec(wr_lo.shape), _const_spec((1, LANES))],
        out_specs=[row, row, rt, pl.BlockSpec((1, 1, LANES), lambda i: (i, 0, 0))],
        out_shape=[act, act, jax.ShapeDtypeStruct((t, LANES), F32),
                   jax.ShapeDtypeStruct((t // ROW_TILE, 1, LANES), F32)],
        compiler_params=_params("parallel"),
        name="proj_route",
    )(a, x, w, nw, wr_hi, wr_lo, br)


def _router_weights(w_group, b_group, w_expert, b_expert):
    pad = LANES - N_GROUPS - N_EXPERTS
    w = jnp.concatenate([w_group, w_expert, jnp.zeros((D_MODEL, pad), F32)], axis=1)
    b = jnp.concatenate([b_group, b_expert, jnp.zeros((pad,), F32)])[None, :]
    hi = w.astype(BF16)
    lo = (w - hi.astype(F32)).astype(BF16)
    return hi, lo, b


def _dispatch_kernel(dest_ref, seg_ref, nused_ref, hn_ref, xbuf_hbm, zero_ref, sem, zsem):
    def zero_row(r):
        return pltpu.make_async_copy(zero_ref.at[pl.ds(0, 1)], xbuf_hbm.at[pl.ds(r, 1)], zsem)

    def zero_block(b):
        return pltpu.make_async_copy(zero_ref, xbuf_hbm.at[pl.ds(pl.multiple_of(b * ROW_BLOCK, ROW_BLOCK), ROW_BLOCK)], zsem)

    def for_each_padding_row(fn):
        def per_expert(e, carry):
            def row(r, c):
                fn(r)
                return c
            return lax.fori_loop(seg_ref[0, e], seg_ref[1, e], row, carry)
        lax.fori_loop(0, N_EXPERTS, per_expert, 0)

    def for_each_unused_block(fn):
        def block(b, c):
            fn(b)
            return c
        lax.fori_loop(nused_ref[0], xbuf_hbm.shape[0] // ROW_BLOCK, block, 0)

    @pl.when(pl.program_id(0) == 0)
    def _():
        zero_ref[...] = jnp.zeros_like(zero_ref)
        for_each_padding_row(lambda r: zero_row(r).start())
        for_each_unused_block(lambda b: zero_block(b).start())
        for_each_padding_row(lambda r: zero_row(r).wait())
        for_each_unused_block(lambda b: zero_block(b).wait())

    def issue(r, carry):
        for k in range(2):
            pltpu.make_async_copy(hn_ref.at[pl.ds(r, 1)], xbuf_hbm.at[pl.ds(dest_ref[0, 0, 2 * r + k], 1)], sem).start()
        return carry

    lax.fori_loop(0, ROW_TILE, issue, 0, unroll=ISSUE_UNROLL)
    for k in range(2):
        pltpu.make_async_copy(hn_ref, xbuf_hbm.at[pl.ds(0, ROW_TILE)], sem).wait()


def _dispatch(dest3, seg, nused, hn, cap):
    t = hn.shape[0]
    return pl.pallas_call(
        _dispatch_kernel,
        grid=(t // ROW_TILE,),
        in_specs=[pl.BlockSpec((1, 1, 2 * ROW_TILE), lambda i: (i, 0, 0), memory_space=pltpu.SMEM),
                  pl.BlockSpec(memory_space=pltpu.SMEM), pl.BlockSpec(memory_space=pltpu.SMEM),
                  pl.BlockSpec((ROW_TILE, D_MODEL), lambda i: (i, 0))],
        out_specs=pl.BlockSpec(memory_space=pl.ANY),
        out_shape=jax.ShapeDtypeStruct((cap, D_MODEL), F32),
        scratch_shapes=[pltpu.VMEM((ROW_BLOCK, D_MODEL), F32), pltpu.SemaphoreType.DMA, pltpu.SemaphoreType.DMA],
        compiler_params=pltpu.CompilerParams(dimension_semantics=("arbitrary",), has_side_effects=True),
        name="moe_dispatch",
    )(dest3, seg, nused, hn)


def _experts_kernel(be_ref, nused_ref, x_ref, wi_ref, wo_ref, y_ref, wi_bf, wo_bf):
    i = pl.program_id(0)
    used = i < nused_ref[0]
    new_expert = (i == 0) | (be_ref[i] != be_ref[jnp.maximum(i - 1, 0)])

    @pl.when(used & new_expert)
    def _():
        wi_bf[...] = wi_ref[0, 0].astype(BF16)
        wo_bf[...] = wo_ref[0, 0].astype(BF16)

    @pl.when(used)
    def _():
        hu = jnp.dot(x_ref[...].astype(BF16), wi_bf[...], preferred_element_type=F32)
        u = hu[:, :EXPERT_FF]
        act = u * _sigmoid(u) * hu[:, EXPERT_FF:]
        y_ref[...] = jnp.dot(act.astype(BF16), wo_bf[...], preferred_element_type=F32)

    @pl.when(jnp.logical_not(used))
    def _():
        y_ref[...] = jnp.zeros_like(y_ref)


def _experts(block_expert, nused, xbuf, w_in, w_out, layer):
    cap = xbuf.shape[0]
    grid_spec = pltpu.PrefetchScalarGridSpec(
        num_scalar_prefetch=2,
        grid=(cap // ROW_BLOCK,),
        in_specs=[pl.BlockSpec((ROW_BLOCK, D_MODEL), lambda i, be, nu: (jnp.minimum(i, nu[0] - 1), 0)),
                  pl.BlockSpec((1, 1, D_MODEL, 2 * EXPERT_FF), lambda i, be, nu: (layer, be[i], 0, 0)),
                  pl.BlockSpec((1, 1, EXPERT_FF, D_MODEL), lambda i, be, nu: (layer, be[i], 0, 0))],
        out_specs=pl.BlockSpec((ROW_BLOCK, D_MODEL), lambda i, be, nu: (i, 0)),
        scratch_shapes=[pltpu.VMEM((D_MODEL, 2 * EXPERT_FF), BF16), pltpu.VMEM((EXPERT_FF, D_MODEL), BF16)],
    )
    return pl.pallas_call(
        _experts_kernel,
        grid_spec=grid_spec,
        out_shape=jax.ShapeDtypeStruct((cap, D_MODEL), F32),
        compiler_params=_params("arbitrary"),
        name="moe_experts",
    )(block_expert, nused, xbuf, w_in, w_out)


def _combine_kernel(dest_ref, dest_next_ref, rt_ref, x_ref, y_hbm, o_ref, buf, sem):
    i = pl.program_id(0)
    slot = i % 2

    def gather(dref, s):
        def issue(r, carry):
            for k in range(2):
                pltpu.make_async_copy(y_hbm.at[pl.ds(dref[0, 0, 2 * r + k], 1)], buf.at[s, k, pl.ds(r, 1)],
                                      sem.at[s]).start()
            return carry
        lax.fori_loop(0, ROW_TILE, issue, 0, unroll=ISSUE_UNROLL)

    @pl.when(i == 0)
    def _():
        gather(dest_ref, 0)

    @pl.when(i + 1 < pl.num_programs(0))
    def _():
        gather(dest_next_ref, 1 - slot)

    for k in range(2):
        pltpu.make_async_copy(y_hbm.at[pl.ds(0, ROW_TILE)], buf.at[slot, k], sem.at[slot]).wait()
    rt = rt_ref[...]
    o_ref[...] = x_ref[...] + rt[:, 2:3] * buf[slot, 0] + rt[:, 3:4] * buf[slot, 1]


def _combine(dest3, route, x, ybuf):
    t = x.shape[0]
    nt = t // ROW_TILE
    row = pl.BlockSpec((ROW_TILE, D_MODEL), lambda i: (i, 0))
    return pl.pallas_call(
        _combine_kernel,
        grid=(nt,),
        in_specs=[pl.BlockSpec((1, 1, 2 * ROW_TILE), lambda i: (i, 0, 0), memory_space=pltpu.SMEM),
                  pl.BlockSpec((1, 1, 2 * ROW_TILE), lambda i: (jnp.minimum(i + 1, nt - 1), 0, 0),
                               memory_space=pltpu.SMEM),
                  pl.BlockSpec((ROW_TILE, LANES), lambda i: (i, 0)), row,
                  pl.BlockSpec(memory_space=pl.ANY)],
        out_specs=row,
        out_shape=jax.ShapeDtypeStruct((t, D_MODEL), F32),
        scratch_shapes=[pltpu.VMEM((2, 2, ROW_TILE, D_MODEL), F32), pltpu.SemaphoreType.DMA((2,))],
        compiler_params=_params("arbitrary"),
        name="moe_combine",
    )(dest3, dest3, route, x, ybuf)


def _moe(x, hn, route, counts, w_in, w_out, layer):
    t = x.shape[0]
    n_rb = -(-2 * t // ROW_BLOCK) + N_EXPERTS
    cap = n_rb * ROW_BLOCK
    cnt = counts[:, 0, :N_EXPERTS].astype(jnp.int32)
    total = jnp.sum(cnt, axis=0)
    padded = (total + ROW_BLOCK - 1) // ROW_BLOCK * ROW_BLOCK
    pad_end = jnp.cumsum(padded)
    pad_start = pad_end - padded
    tile_start = pad_start[None, :] + jnp.cumsum(cnt, axis=0) - cnt
    ids = route[:, 0:2].astype(jnp.int32)
    picked = ids[:, :, None] == jnp.arange(N_EXPERTS, dtype=jnp.int32)[None, None, :]
    starts = jnp.repeat(tile_start, ROW_TILE, axis=0)[:, None, :]
    dest = jnp.sum(jnp.where(picked, starts, 0), axis=-1) + route[:, 4:6].astype(jnp.int32)
    dest3 = dest.reshape(t // ROW_TILE, 1, 2 * ROW_TILE)
    seg = jnp.stack([pad_start + total, pad_end]).astype(jnp.int32)
    block_row = jnp.arange(n_rb, dtype=jnp.int32) * ROW_BLOCK
    block_expert = jnp.minimum(jnp.sum(pad_end[None, :] <= block_row[:, None], axis=1), N_EXPERTS - 1).astype(jnp.int32)
    nused = (pad_end[-1:] // ROW_BLOCK).astype(jnp.int32)
    xbuf = _dispatch(dest3, seg, nused, hn, cap)
    ybuf = _experts(block_expert, nused, xbuf, w_in, w_out, layer)
    return _combine(dest3, route, x, ybuf)


def _kvq_kernel(x_ref, kvn_ref, qn_ref, wk_ref, wvt_ref, wqt_ref, knw_ref, qnw_ref, k_ref, vt_ref, qt_ref, km_ref):
    x = x_ref[...]
    hkv = _rms(x, kvn_ref[...]).astype(BF16)
    k = _head_norm(jnp.dot(hkv, wk_ref[...], preferred_element_type=F32), knw_ref[...])
    k_ref[...] = k.astype(k_ref.dtype)
    km_ref[0] = jnp.mean(k, axis=0, keepdims=True)
    vt = _dot_nt(wvt_ref[...], hkv)
    hq = _rms(x, qn_ref[...]).astype(BF16)
    qt = _dot_nt(wqt_ref[...], hq)
    qnw = qnw_ref[...]
    for h in range(N_HEADS):
        hs = slice(h * HEAD_DIM, (h + 1) * HEAD_DIM)
        vt_ref[0, h, 0, :HEAD_DIM] = vt[hs].astype(vt_ref.dtype)
        vt_ref[0, h, 0, HEAD_DIM:] = jnp.ones((V_ROWS - HEAD_DIM, MOBA_BLOCK), vt_ref.dtype)
        seg = qt[hs]
        qt_ref[0, h, 0] = seg * lax.rsqrt(jnp.mean(seg * seg, axis=0, keepdims=True) + EPS) * qnw


def _kvq(x, kv_norm, q_prenorm, w_kv, w_q, k_norm, q_norm, batch, nb):
    t = x.shape[0]
    row = pl.BlockSpec((MOBA_BLOCK, D_MODEL), lambda i: (i, 0))
    def tblk(rows):
        return pl.BlockSpec((1, N_HEADS, 1, rows, MOBA_BLOCK), lambda i: (i // nb, 0, i % nb, 0, 0))

    def tshape(rows):
        return (batch, N_HEADS, nb, rows, MOBA_BLOCK)

    w_k = w_kv[:, :D_MODEL].astype(BF16)
    w_vt = w_kv[:, D_MODEL:].T.astype(BF16)
    w_qt = w_q.T.astype(BF16)
    qnw = jnp.broadcast_to(q_norm.reshape(HEAD_DIM, 1), (HEAD_DIM, MOBA_BLOCK))
    wspec = _const_spec((D_MODEL, D_MODEL))
    return pl.pallas_call(
        _kvq_kernel,
        grid=(t // MOBA_BLOCK,),
        in_specs=[row, _const_spec((1, D_MODEL)), _const_spec((1, D_MODEL)), wspec, wspec, wspec,
                  _const_spec((1, HEAD_DIM)), _const_spec((HEAD_DIM, MOBA_BLOCK))],
        out_specs=[row, tblk(V_ROWS), tblk(HEAD_DIM), pl.BlockSpec((1, 1, D_MODEL), lambda i: (i, 0, 0))],
        out_shape=[jax.ShapeDtypeStruct((t, D_MODEL), BF16), jax.ShapeDtypeStruct(tshape(V_ROWS), BF16),
                   jax.ShapeDtypeStruct(tshape(HEAD_DIM), F32),
                   jax.ShapeDtypeStruct((t // MOBA_BLOCK, 1, D_MODEL), F32)],
        compiler_params=_params("parallel"),
        name="kvq",
    )(x, kv_norm, q_prenorm, w_k, w_vt, w_qt, k_norm, qnw)


def _bias_tiles_kernel(rb_ref, o_ref):
    h, delta = pl.program_id(0), pl.program_id(1)
    shape = (MOBA_BLOCK, MOBA_BLOCK)
    dist = delta * MOBA_BLOCK + lax.broadcasted_iota(jnp.int32, shape, 1) - lax.broadcasted_iota(jnp.int32, shape, 0)
    n = jnp.maximum(dist, 0)
    max_exact = NUM_BUCKETS // 2
    log_ratio = (jnp.log(jnp.maximum(n, max_exact).astype(F32) / max_exact)
                 / math.log(MAX_DISTANCE / max_exact))
    large = max_exact + (log_ratio * (NUM_BUCKETS - max_exact)).astype(jnp.int32)
    bucket = jnp.where(n < max_exact, n, jnp.minimum(large, NUM_BUCKETS - 1))
    bias = jnp.zeros(shape, F32)
    for b in range(NUM_BUCKETS):
        bias = jnp.where(bucket == b, rb_ref[b, h], bias)
    o_ref[0, 0] = jnp.where(dist >= 0, bias * LOG2E, NEG)


def _bias_tiles(rel_bias):
    return pl.pallas_call(
        _bias_tiles_kernel,
        grid=(N_HEADS, N_NEAR + 1),
        in_specs=[pl.BlockSpec(memory_space=pltpu.SMEM)],
        out_specs=pl.BlockSpec((1, 1, MOBA_BLOCK, MOBA_BLOCK), lambda h, d: (h, d, 0, 0)),
        out_shape=jax.ShapeDtypeStruct((N_HEADS, N_NEAR + 1, MOBA_BLOCK, MOBA_BLOCK), F32),
        compiler_params=_params("parallel", "parallel"),
        name="bias_tiles",
    )(rel_bias)


def _select_kernel(qt_ref, km_ref, o_ref):
    i = pl.program_id(1)
    nb = km_ref.shape[1]
    km = km_ref[0]
    blk = lax.broadcasted_iota(jnp.int32, (nb, MOBA_BLOCK), 0)
    blkf = blk.astype(F32)
    for h in range(N_HEADS):
        gate = jnp.dot(km[:, h * HEAD_DIM:(h + 1) * HEAD_DIM], qt_ref[0, h, 0],
                       precision=HIGHEST, preferred_element_type=F32)
        g = jnp.where(blk < i, gate, NEG)
        mask = jnp.full((nb, MOBA_BLOCK), NEG, F32)
        for _ in range(MOBA_TOPK):
            m = jnp.max(g, axis=0, keepdims=True)
            idx = jnp.min(jnp.where(g == m, blkf, float(nb)), axis=0, keepdims=True)
            hit = (blkf == idx) & (m > 0.5 * NEG)
            mask = jnp.where(hit, 0.0, mask)
            g = jnp.where(blkf == idx, NEG, g)
        o_ref[0, h, 0] = jnp.where(blk == i, 0.0, mask)


def _select(qt, kmean, batch, nb):
    tblk = pl.BlockSpec((1, N_HEADS, 1, HEAD_DIM, MOBA_BLOCK), lambda b, i: (b, 0, i, 0, 0))
    return pl.pallas_call(
        _select_kernel,
        grid=(batch, nb),
        in_specs=[tblk, pl.BlockSpec((1, nb, D_MODEL), lambda b, i: (b, 0, 0))],
        out_specs=pl.BlockSpec((1, N_HEADS, 1, nb, MOBA_BLOCK), lambda b, i: (b, 0, i, 0, 0)),
        out_shape=jax.ShapeDtypeStruct((batch, N_HEADS, nb, nb, MOBA_BLOCK), F32),
        compiler_params=_params("parallel", "parallel"),
        name="moba_select",
    )(qt, kmean)


def _attention_kernel(qt_ref, k_ref, vt_ref, mask_ref, bias_ref, o_ref, acc_ref, ss_ref, ps_ref):
    i = pl.program_id(2)
    blk = MOBA_BLOCK
    nbatch = qt_ref.shape[0]
    qts = [(qt_ref[b, 0, 0] * (ATT_SCALE * LOG2E)).astype(BF16) for b in range(nbatch)]
    far_bias = bias_ref[0, N_NEAR, 0:1, :]

    def block_of(t):
        near = jnp.where(t < ATT_NEAR_STEPS, i - t, t - ATT_NEAR_STEPS)
        return jnp.where((t == 0) | (t > i), i, near)

    def stage_s(t, slot):
        j = block_of(t)
        for b in range(nbatch):
            ss_ref[slot, b] = jnp.dot(k_ref[b, pl.ds(pl.multiple_of(j * blk, blk), blk), :], qts[b],
                                      preferred_element_type=F32)

    def stage_f(t, slot, ms, near):
        j = block_of(t)
        dead = jnp.where(t > i, NEG, 0.0)
        if near:
            bias = bias_ref[0, jnp.minimum(i - j, N_NEAR)]
        new_m, alphas = [], []
        for b in range(nbatch):
            s = ss_ref[slot, b]
            rowb = mask_ref[b, 0, 0, pl.ds(j, 1), :] + dead
            if near:
                s = s + bias
            else:
                rowb = rowb + far_bias
            mc = jnp.maximum(ms[b], jnp.max(s, axis=0, keepdims=True) + rowb)
            ps_ref[slot, b] = jnp.exp2(s - (mc - rowb)).astype(BF16)
            alphas.append(jnp.exp2(ms[b] - mc))
            new_m.append(mc)
        return tuple(new_m), tuple(alphas)

    def stage_a(t, slot, alphas):
        j = block_of(t)
        pvs = [jnp.dot(vt_ref[b, 0, j], ps_ref[slot, b], preferred_element_type=F32) for b in range(nbatch)]
        for b in range(nbatch):
            acc_ref[b] = alphas[b] * acc_ref[b] + pvs[b]

    stage_s(0, 0)
    ps_ref[1] = jnp.zeros(ps_ref.shape[1:], BF16)
    acc_ref[...] = jnp.zeros_like(acc_ref)
    row = (1, blk)
    init = (tuple(jnp.full(row, NEG, F32) for _ in range(nbatch)), tuple(jnp.ones(row, F32) for _ in range(nbatch)))

    def two_steps(tt, carry, near):
        ms, alphas = carry
        t = 2 * tt
        stage_s(t + 1, 1)
        ms, alphas0 = stage_f(t, 0, ms, near)
        stage_a(jnp.maximum(t - 1, 0), 1, alphas)
        stage_s(t + 2, 0)
        ms, alphas1 = stage_f(t + 1, 1, ms, near)
        stage_a(t, 0, alphas0)
        return ms, alphas1

    trips = (i + 2) // 2
    near_trips = jnp.minimum(trips, ATT_NEAR_STEPS // 2)
    carry = lax.fori_loop(0, near_trips, functools.partial(two_steps, near=True), init)
    ms, alphas = lax.fori_loop(near_trips, trips, functools.partial(two_steps, near=False), carry)
    stage_a(2 * trips - 1, 1, alphas)
    for b in range(nbatch):
        acc = acc_ref[b]
        o_ref[b] = (acc[:HEAD_DIM] / acc[HEAD_DIM:HEAD_DIM + 1]).T.astype(o_ref.dtype)


def _attention(qt, kn, vt, mask, bias, batch, seq):
    nb = seq // MOBA_BLOCK
    bg = max(c for c in range(1, ATT_BATCH_GROUP + 1) if batch % c == 0)
    once = pl.Buffered(1)
    tile = (2, bg, MOBA_BLOCK, MOBA_BLOCK)
    out = pl.pallas_call(
        _attention_kernel,
        grid=(batch // bg, N_HEADS, nb),
        in_specs=[pl.BlockSpec((bg, 1, 1, HEAD_DIM, MOBA_BLOCK), lambda g, h, i: (g, h, i, 0, 0)),
                  pl.BlockSpec((bg, seq, HEAD_DIM), lambda g, h, i: (g, 0, h), pipeline_mode=once),
                  pl.BlockSpec((bg, 1, nb, V_ROWS, MOBA_BLOCK), lambda g, h, i: (g, h, 0, 0, 0), pipeline_mode=once),
                  pl.BlockSpec((bg, 1, 1, nb, MOBA_BLOCK), lambda g, h, i: (g, h, i, 0, 0)),
                  pl.BlockSpec((1, N_NEAR + 1, MOBA_BLOCK, MOBA_BLOCK), lambda g, h, i: (h, 0, 0, 0))],
        out_specs=pl.BlockSpec((bg, MOBA_BLOCK, HEAD_DIM), lambda g, h, i: (g, i, h)),
        out_shape=jax.ShapeDtypeStruct((batch, seq, D_MODEL), BF16),
        scratch_shapes=[pltpu.VMEM((bg, V_ROWS, MOBA_BLOCK), F32), pltpu.VMEM(tile, F32), pltpu.VMEM(tile, BF16)],
        compiler_params=_params("parallel", "parallel", "arbitrary"),
        name="moba_attention",
    )(qt, kn.reshape(batch, seq, D_MODEL), vt, mask, bias)
    return out.reshape(batch * seq, D_MODEL)


def kernel(x, mix_norm, ffn_norm, hg_w_in, hg_lb, hg_o_norm, hg_w_out, kv_norm, w_kv, k_norm, att_w_q, q_norm, att_w_o, rel_bias, moe_w_group, moe_b_group, moe_w_expert, moe_b_expert, moe_w_in, moe_w_out):
    batch, seq, d = x.shape
    assert d == D_MODEL and seq % GLA_TILE == 0 and seq % MOBA_BLOCK == 0
    assert mix_norm.shape[0] == 2 and hg_w_in.shape[0] == 1 and att_w_q.shape[0] == 1
    x0 = x.reshape(batch * seq, d)

    q, f, v, g = _hg_proj(x0, mix_norm[0:1], hg_lb, hg_w_in[0].astype(BF16))
    og = _gla(q, f, v, g, hg_o_norm[0:1], batch, seq)
    x1, hn, route, counts = _proj_route(
        og, x0, hg_w_out[0].astype(BF16), ffn_norm[0:1],
        *_router_weights(moe_w_group[0], moe_b_group[0], moe_w_expert[0], moe_b_expert[0]))
    x2 = _moe(x1, hn, route, counts, moe_w_in, moe_w_out, 0)

    nb = seq // MOBA_BLOCK
    kn, vt, qt, kmean = _kvq(x2, kv_norm[None, :], mix_norm[1:2], w_kv, att_w_q[0], k_norm[None, :], q_norm[0],
                             batch, nb)
    mask = _select(qt, kmean.reshape(batch, nb, d), batch, nb)
    att = _attention(qt, kn, vt, mask, _bias_tiles(rel_bias), batch, seq)
    x3, hn, route, counts = _proj_route(
        att, x2, att_w_o[0].astype(BF16), ffn_norm[1:2],
        *_router_weights(moe_w_group[1], moe_b_group[1], moe_w_expert[1], moe_b_expert[1]))
    out = _moe(x3, hn, route, counts, moe_w_in, moe_w_out, 1)
    return out.reshape(batch, seq, d)
```

```python
import functools
import math

import jax
import jax.numpy as jnp
import numpy as np
from jax import lax
from jax.experimental import pallas as pl
from jax.experimental.pallas import tpu as pltpu

F32 = jnp.float32
BF16 = jnp.bfloat16
HIGHEST = lax.Precision.HIGHEST

D_MODEL = 1024
N_HEADS = 8
HEAD_DIM = 128
HG_CHUNK = 64
ATT_SCALE = HEAD_DIM ** -0.5
MOBA_BLOCK = 256
MOBA_TOPK = 3
NUM_BUCKETS = 32
MAX_DISTANCE = 1024
N_GROUPS = 4
EXPERTS_PER_GROUP = 8
N_EXPERTS = N_GROUPS * EXPERTS_PER_GROUP
EXPERT_FF = D_MODEL // 2
EPS = 1e-6
NEG = -1e30
LOG2E = math.log2(math.e)
ATT_BATCH_GROUP = 4

LANES = 128
SUBLANES = 8
ROW_AS_TILE = (SUBLANES, LANES)
assert SUBLANES * LANES == D_MODEL
ROW_TILE = 512
GLA_TILE = 512
KV_TILE = 512
ROW_BLOCK = 512
ISSUE_UNROLL = 8
DISPATCH_SLOTS = 3
VMEM_LIMIT = 56 * 1024 * 1024


def _near_block_count():
    max_exact = NUM_BUCKETS // 2
    delta = 1
    while True:
        dist = delta * MOBA_BLOCK - (MOBA_BLOCK - 1)
        steps = (math.log(max(dist, max_exact) / max_exact) / math.log(MAX_DISTANCE / max_exact)
                 * (NUM_BUCKETS - max_exact))
        if steps >= NUM_BUCKETS - 1 - max_exact + 0.5:
            return delta
        delta += 1


N_NEAR = _near_block_count()
ATT_NEAR_STEPS = N_NEAR + 1 + (N_NEAR + 1) % 2
V_ROWS = HEAD_DIM + 16


def _params(*sem):
    return pltpu.CompilerParams(dimension_semantics=sem, vmem_limit_bytes=VMEM_LIMIT)


def _const_spec(shape):
    nd = len(shape)
    return pl.BlockSpec(shape, lambda *_: (0,) * nd, pipeline_mode=pl.Buffered(1))


def _rms(x, w):
    return x * lax.rsqrt(jnp.mean(x * x, axis=-1, keepdims=True) + EPS) * w


def _sigmoid(x):
    return 1.0 / (1.0 + jnp.exp(-x))


def _rows_to_tiles(value):
    cols = jnp.stack([value[:, c * LANES:(c + 1) * LANES] for c in range(SUBLANES)])
    return pltpu.einshape("crl->rcl", cols)


def _tiles_to_cols(tiles):
    return pltpu.einshape("rcl->crl", tiles)


def _dot_nt(a, b):
    return lax.dot_general(a, b, (((1,), (1,)), ((), ())), preferred_element_type=F32)


def _head_norm(x, w):
    segs = [_rms(x[:, h * HEAD_DIM:(h + 1) * HEAD_DIM], w) for h in range(N_HEADS)]
    return jnp.concatenate(segs, axis=-1)


def _hg_proj_kernel(x_ref, nw_ref, lb_ref, w_ref, q_ref, f_ref, v_ref, g_ref):
    h = _rms(x_ref[...], nw_ref[...]).astype(BF16)
    lbl = lb_ref[...]
    e = jnp.exp(lbl - jnp.max(lbl, axis=0, keepdims=True))
    lb = e[0:1] / jnp.sum(e, axis=0, keepdims=True)
    d = D_MODEL
    q = jnp.dot(h, w_ref[:, 0:d], preferred_element_type=F32)
    q_ref[...] = q * (HEAD_DIM ** -0.5)
    fl = jnp.dot(h, w_ref[:, d:2 * d], preferred_element_type=F32)
    f_ref[...] = lb + (1.0 - lb) * _sigmoid(fl)
    v_ref[...] = jnp.dot(h, w_ref[:, 2 * d:3 * d], preferred_element_type=F32)
    g = jnp.dot(h, w_ref[:, 3 * d:4 * d], preferred_element_type=F32)
    g_ref[...] = g * _sigmoid(g)


def _hg_proj(x, nw, hg_lb, w_in):
    t = x.shape[0]
    out = jax.ShapeDtypeStruct((t, D_MODEL), F32)
    row = pl.BlockSpec((ROW_TILE, D_MODEL), lambda i: (i, 0))
    return pl.pallas_call(
        _hg_proj_kernel,
        grid=(t // ROW_TILE,),
        in_specs=[row, _const_spec((1, D_MODEL)), _const_spec(hg_lb.shape), _const_spec(w_in.shape)],
        out_specs=[row, row, row, row],
        out_shape=[out, out, out, out],
        compiler_params=_params("parallel"),
        name="hg_proj",
    )(x, nw, hg_lb, w_in)


def _gla_level_tables():
    c = HG_CHUNK
    t = np.arange(c)[:, None]
    s = np.arange(c)[None, :]
    cum, mask, refs = [s <= t], [s == t], []
    h = c // 2
    while h >= 1:
        mid = t // (2 * h) * (2 * h) + h
        if h >= SUBLANES:
            refs.append(("rows", [int(mid[r, 0]) - 1 for r in range(0, c, SUBLANES)]))
        else:
            refs.append(("cum", len(cum)))
            cum.append(s <= mid - 1)
        mask.append((t // (2 * h) == s // (2 * h)) & (t % (2 * h) >= h) & (s % (2 * h) < h))
        h //= 2
    cum = np.concatenate(cum, 0).astype(np.float32)
    return np.tile(cum, (1, 3)), np.stack(mask).astype(np.float32), refs


def _gla_kernel(level_refs, q_ref, f_ref, v_ref, g_ref, nw_ref, cum_ref, msk_ref, o_ref, st_ref):
    @pl.when(pl.program_id(1) == 0)
    def _():
        st_ref[...] = jnp.zeros_like(st_ref)

    c = HG_CHUNK
    nw = nw_ref[...]
    heads = [slice(h * HEAD_DIM, (h + 1) * HEAD_DIM) for h in range(N_HEADS)]

    def chunk(ci, carry):
        rs = pl.ds(pl.multiple_of(ci * c, c), c)
        cum = cum_ref[...]
        qs, ks, vs, bbs, scs = [], [], [], [], []
        for hs in heads:
            f = f_ref[rs, hs]
            lf = jnp.log(f) * LOG2E
            l1 = lf.astype(BF16)
            r1 = lf - l1.astype(F32)
            l2 = r1.astype(BF16)
            l3 = (r1 - l2.astype(F32)).astype(BF16)
            bbs.append(jnp.dot(cum, jnp.concatenate([l1, l2, l3], axis=0), preferred_element_type=F32))
            qs.append(q_ref[rs, hs])
            ks.append(1.0 - f)
            vs.append(v_ref[rs, hs].astype(BF16))
        for h in range(N_HEADS):
            q, k, bb = qs[h], ks[h], bbs[h]
            b = bb[0:c]
            a = _dot_nt(q.astype(BF16), k.astype(BF16)) * msk_ref[0]
            for lev, (kind, where) in enumerate(level_refs, start=1):
                if kind == "rows":
                    bref = jnp.concatenate([jnp.broadcast_to(b[r:r + 1], (SUBLANES, HEAD_DIM)) for r in where], axis=0)
                else:
                    bref = bb[where * c:(where + 1) * c]
                e = jnp.exp2(-jnp.abs(b - bref))
                a = a + _dot_nt((q * e).astype(BF16), (k * e).astype(BF16)) * msk_ref[lev]
            scs.append(a.astype(BF16))
        outs = []
        for h in range(N_HEADS):
            q, k, v, b = qs[h], ks[h], vs[h], bbs[h][0:c]
            st = st_ref[h]
            o = jnp.dot(scs[h], v, preferred_element_type=F32)
            o = o + _dot_nt((q * jnp.exp2(b)).astype(BF16), st.astype(BF16))
            bl = b[c - 1:c]
            kd = (k * jnp.exp2(bl - b)).astype(BF16)
            upd = lax.dot_general(v, kd, (((0,), (0,)), ((), ())), preferred_element_type=F32)
            st_ref[h] = st * jnp.exp2(bl) + upd
            outs.append(o)
        for h, hs in enumerate(heads):
            o_ref[rs, hs] = (_rms(outs[h], nw) * g_ref[rs, hs]).astype(o_ref.dtype)
        return carry

    lax.fori_loop(0, GLA_TILE // c, chunk, 0)


def _gla(q, f, v, g, nw, batch, seq):
    t = q.shape[0]
    nl = seq // GLA_TILE
    cum, msk, level_refs = _gla_level_tables()
    blk = pl.BlockSpec((GLA_TILE, D_MODEL), lambda b, l: (b * nl + l, 0))
    return pl.pallas_call(
        functools.partial(_gla_kernel, level_refs),
        grid=(batch, nl),
        in_specs=[blk, blk, blk, blk, _const_spec((1, HEAD_DIM)), _const_spec(cum.shape), _const_spec(msk.shape)],
        out_specs=blk,
        out_shape=jax.ShapeDtypeStruct((t, D_MODEL), BF16),
        scratch_shapes=[pltpu.VMEM((N_HEADS, HEAD_DIM, HEAD_DIM), F32)],
        compiler_params=_params("parallel", "arbitrary"),
        name="gla",
    )(q, f, v, g, nw, jnp.asarray(cum, BF16), jnp.asarray(msk, F32))


def _proj_route_kernel(a_ref, x_ref, w_ref, nw_ref, wrh_ref, wrl_ref, br_ref, xo_ref, hn_ref, rt_ref, cnt_ref):
    y = x_ref[...] + jnp.dot(a_ref[...], w_ref[...], preferred_element_type=F32)
    xo_ref[...] = y
    hn = _rms(y, nw_ref[...])
    hn_ref[...] = _rows_to_tiles(hn)
    hh = hn.astype(BF16)
    hl = (hn - hh.astype(F32)).astype(BF16)
    logits = (jnp.dot(hh, wrh_ref[...], preferred_element_type=F32)
              + jnp.dot(hl, wrh_ref[...], preferred_element_type=F32)
              + jnp.dot(hh, wrl_ref[...], preferred_element_type=F32)) + br_ref[...]
    lane = lax.broadcasted_iota(jnp.int32, logits.shape, 1)
    lanef = lane.astype(F32)
    big = float(LANES)
    gl = jnp.where(lane < N_GROUPS, logits, NEG)
    gmax = jnp.max(gl, axis=-1, keepdims=True)
    gidx = jnp.min(jnp.where(gl == gmax, lanef, big), axis=-1, keepdims=True)
    g_w = 1.0 / jnp.sum(jnp.exp(gl - gmax), axis=-1, keepdims=True)
    lo = N_GROUPS + gidx * EXPERTS_PER_GROUP
    el = jnp.where((lanef >= lo) & (lanef < lo + EXPERTS_PER_GROUP), logits, NEG)
    m1 = jnp.max(el, axis=-1, keepdims=True)
    i1 = jnp.min(jnp.where(el == m1, lanef, big), axis=-1, keepdims=True)
    el2 = jnp.where(lanef == i1, NEG, el)
    m2 = jnp.max(el2, axis=-1, keepdims=True)
    i2 = jnp.min(jnp.where(el2 == m2, lanef, big), axis=-1, keepdims=True)
    e2 = jnp.exp(m2 - m1)
    w1 = g_w / (1.0 + e2)
    w2 = g_w * e2 / (1.0 + e2)
    ex1, ex2 = i1 - N_GROUPS, i2 - N_GROUPS
    oh1 = jnp.where(lanef == ex1, 1.0, 0.0)
    oh2 = jnp.where(lanef == ex2, 1.0, 0.0)
    n = logits.shape[0]
    before = jnp.where(lax.broadcasted_iota(jnp.int32, (n, n), 0) > lax.broadcasted_iota(jnp.int32, (n, n), 1),
                       1.0, 0.0).astype(BF16)
    c1 = jnp.dot(before, oh1.astype(BF16), preferred_element_type=F32)
    c2 = jnp.dot(before, oh2.astype(BF16), preferred_element_type=F32)
    tot1 = jnp.sum(oh1, axis=0, keepdims=True)
    r1 = jnp.sum(oh1 * c1, axis=-1, keepdims=True)
    r2 = jnp.sum(oh2 * (c2 + tot1), axis=-1, keepdims=True)
    cnt_ref[0] = tot1 + jnp.sum(oh2, axis=0, keepdims=True)
    rt = jnp.where(lane == 0, ex1, 0.0)
    rt = jnp.where(lane == 1, ex2, rt)
    rt = jnp.where(lane == 2, w1, rt)
    rt = jnp.where(lane == 3, w2, rt)
    rt = jnp.where(lane == 4, r1, rt)
    rt_ref[...] = jnp.where(lane == 5, r2, rt)


def _proj_route(a, x, w, nw, wr_hi, wr_lo, br):
    t = x.shape[0]
    row = pl.BlockSpec((ROW_TILE, D_MODEL), lambda i: (i, 0))
    rt = pl.BlockSpec((ROW_TILE, LANES), lambda i: (i, 0))
    act = jax.ShapeDtypeStruct((t, D_MODEL), F32)
    return pl.pallas_call(
        _proj_route_kernel,
        grid=(t // ROW_TILE,),
        in_specs=[row, row, _const_spec(w.shape), _const_spec((1, D_MODEL)),
                  _const_spec(wr_hi.shape), _const_spec(wr_lo.shape), _const_spec((1, LANES))],
        out_specs=[row, pl.BlockSpec((ROW_TILE,) + ROW_AS_TILE, lambda i: (i, 0, 0)), rt,
                   pl.BlockSpec((1, 1, LANES), lambda i: (i, 0, 0))],
        out_shape=[act, jax.ShapeDtypeStruct((t,) + ROW_AS_TILE, F32), jax.ShapeDtypeStruct((t, LANES), F32),
                   jax.ShapeDtypeStruct((t // ROW_TILE, 1, LANES), F32)],
        compiler_params=_params("parallel"),
        name="proj_route",
    )(a, x, w, nw, wr_hi, wr_lo, br)


def _router_weights(w_group, b_group, w_expert, b_expert):
    pad = LANES - N_GROUPS - N_EXPERTS
    w = jnp.concatenate([w_group, w_expert, jnp.zeros((D_MODEL, pad), F32)], axis=1)
    b = jnp.concatenate([b_group, b_expert, jnp.zeros((pad,), F32)])[None, :]
    hi = w.astype(BF16)
    lo = (w - hi.astype(F32)).astype(BF16)
    return hi, lo, b


def _dispatch_kernel(dest_ref, seg_ref, nused_ref, hn_hbm, xbuf_hbm, stage, zero_ref, lsem, ssem, zsem):
    i = pl.program_id(0)
    last = pl.num_programs(0) - 1
    slot = i % DISPATCH_SLOTS

    def load(step):
        s = step % DISPATCH_SLOTS
        return pltpu.make_async_copy(hn_hbm.at[pl.ds(pl.multiple_of(step * ROW_TILE, ROW_TILE), ROW_TILE)],
                                     stage.at[s], lsem.at[s])

    def wait_scatter(step):
        s = step % DISPATCH_SLOTS
        for k in range(2):
            pltpu.make_async_copy(stage.at[s], xbuf_hbm.at[pl.ds(0, ROW_TILE)], ssem.at[s]).wait()

    def zero_row(r):
        return pltpu.make_async_copy(zero_ref.at[pl.ds(0, 1)], xbuf_hbm.at[pl.ds(r, 1)], zsem)

    def zero_block(b):
        return pltpu.make_async_copy(zero_ref, xbuf_hbm.at[pl.ds(pl.multiple_of(b * ROW_BLOCK, ROW_BLOCK), ROW_BLOCK)], zsem)

    def for_each_padding_row(fn):
        def per_expert(e, carry):
            def row(r, c):
                fn(r)
                return c
            return lax.fori_loop(seg_ref[0, e], seg_ref[1, e], row, carry)
        lax.fori_loop(0, N_EXPERTS, per_expert, 0)

    def for_each_unused_block(fn):
        def block(b, c):
            fn(b)
            return c
        lax.fori_loop(nused_ref[0], xbuf_hbm.shape[0] // ROW_BLOCK, block, 0)

    @pl.when(i == 0)
    def _():
        load(0).start()
        zero_ref[...] = jnp.zeros_like(zero_ref)
        for_each_padding_row(lambda r: zero_row(r).start())
        for_each_unused_block(lambda b: zero_block(b).start())
        for_each_padding_row(lambda r: zero_row(r).wait())
        for_each_unused_block(lambda b: zero_block(b).wait())

    load(i).wait()

    @pl.when(i < last)
    def _():
        load(i + 1).start()

    def issue(r, carry):
        for k in range(2):
            pltpu.make_async_copy(stage.at[slot, pl.ds(r, 1)], xbuf_hbm.at[pl.ds(dest_ref[0, 0, 2 * r + k], 1)],
                                  ssem.at[slot]).start(priority=k)
        return carry

    lax.fori_loop(0, ROW_TILE, issue, 0, unroll=ISSUE_UNROLL)

    @pl.when(i > 0)
    def _():
        wait_scatter(i - 1)

    @pl.when(i == last)
    def _():
        wait_scatter(i)


def _dispatch(dest3, seg, nused, hn, cap):
    t = hn.shape[0]
    ring = (DISPATCH_SLOTS,)
    return pl.pallas_call(
        _dispatch_kernel,
        grid=(t // ROW_TILE,),
        in_specs=[pl.BlockSpec((1, 1, 2 * ROW_TILE), lambda i: (i, 0, 0), memory_space=pltpu.SMEM),
                  pl.BlockSpec(memory_space=pltpu.SMEM), pl.BlockSpec(memory_space=pltpu.SMEM),
                  pl.BlockSpec(memory_space=pl.ANY)],
        out_specs=pl.BlockSpec(memory_space=pl.ANY),
        out_shape=jax.ShapeDtypeStruct((cap,) + ROW_AS_TILE, F32),
        scratch_shapes=[pltpu.VMEM(ring + (ROW_TILE,) + ROW_AS_TILE, F32), pltpu.VMEM((ROW_BLOCK,) + ROW_AS_TILE, F32),
                        pltpu.SemaphoreType.DMA(ring), pltpu.SemaphoreType.DMA(ring), pltpu.SemaphoreType.DMA],
        compiler_params=pltpu.CompilerParams(dimension_semantics=("arbitrary",), has_side_effects=True),
        name="moe_dispatch",
    )(dest3, seg, nused, hn)


def _experts_kernel(be_ref, nused_ref, x_ref, wi_ref, wo_ref, y_ref, wi_bf, wo_bf):
    i = pl.program_id(0)
    used = i < nused_ref[0]
    new_expert = (i == 0) | (be_ref[i] != be_ref[jnp.maximum(i - 1, 0)])

    @pl.when(used & new_expert)
    def _():
        wi_bf[...] = wi_ref[0, 0].astype(BF16)
        wo_bf[...] = wo_ref[0, 0].astype(BF16)

    @pl.when(used)
    def _():
        cols = _tiles_to_cols(x_ref[...])
        x = jnp.concatenate([cols[c] for c in range(SUBLANES)], axis=-1)
        hu = jnp.dot(x.astype(BF16), wi_bf[...], preferred_element_type=F32)
        u = hu[:, :EXPERT_FF]
        act = u * _sigmoid(u) * hu[:, EXPERT_FF:]
        y_ref[...] = _rows_to_tiles(jnp.dot(act.astype(BF16), wo_bf[...], preferred_element_type=F32))

    @pl.when(jnp.logical_not(used))
    def _():
        y_ref[...] = jnp.zeros_like(y_ref)


def _experts(block_expert, nused, xbuf, w_in, w_out, layer):
    cap = xbuf.shape[0]
    grid_spec = pltpu.PrefetchScalarGridSpec(
        num_scalar_prefetch=2,
        grid=(cap // ROW_BLOCK,),
        in_specs=[pl.BlockSpec((ROW_BLOCK,) + ROW_AS_TILE, lambda i, be, nu: (jnp.minimum(i, nu[0] - 1), 0, 0)),
                  pl.BlockSpec((1, 1, D_MODEL, 2 * EXPERT_FF), lambda i, be, nu: (layer, be[i], 0, 0)),
                  pl.BlockSpec((1, 1, EXPERT_FF, D_MODEL), lambda i, be, nu: (layer, be[i], 0, 0))],
        out_specs=pl.BlockSpec((ROW_BLOCK,) + ROW_AS_TILE, lambda i, be, nu: (i, 0, 0)),
        scratch_shapes=[pltpu.VMEM((D_MODEL, 2 * EXPERT_FF), BF16), pltpu.VMEM((EXPERT_FF, D_MODEL), BF16)],
    )
    return pl.pallas_call(
        _experts_kernel,
        grid_spec=grid_spec,
        out_shape=jax.ShapeDtypeStruct((cap,) + ROW_AS_TILE, F32),
        compiler_params=_params("arbitrary"),
        name="moe_experts",
    )(block_expert, nused, xbuf, w_in, w_out)


def _combine_kernel(dest_ref, dest_next_ref, rt_ref, x_ref, y_hbm, o_ref, buf, sem):
    i = pl.program_id(0)
    slot = i % 2

    def gather(dref, s):
        def issue(r, carry):
            for k in range(2):
                pltpu.make_async_copy(y_hbm.at[pl.ds(dref[0, 0, 2 * r + k], 1)], buf.at[s, k, pl.ds(r, 1)],
                                      sem.at[s]).start(priority=k)
            return carry
        lax.fori_loop(0, ROW_TILE, issue, 0, unroll=ISSUE_UNROLL)

    @pl.when(i == 0)
    def _():
        gather(dest_ref, 0)

    @pl.when(i + 1 < pl.num_programs(0))
    def _():
        gather(dest_next_ref, 1 - slot)

    for k in range(2):
        pltpu.make_async_copy(y_hbm.at[pl.ds(0, ROW_TILE)], buf.at[slot, k], sem.at[slot]).wait()
    rt = rt_ref[...]
    w1, w2 = rt[:, 2:3], rt[:, 3:4]
    g1 = _tiles_to_cols(buf[slot, 0])
    g2 = _tiles_to_cols(buf[slot, 1])
    for c in range(SUBLANES):
        cols = slice(c * LANES, (c + 1) * LANES)
        o_ref[:, cols] = x_ref[:, cols] + w1 * g1[c] + w2 * g2[c]


def _combine(dest3, route, x, ybuf):
    t = x.shape[0]
    nt = t // ROW_TILE
    row = pl.BlockSpec((ROW_TILE, D_MODEL), lambda i: (i, 0))
    return pl.pallas_call(
        _combine_kernel,
        grid=(nt,),
        in_specs=[pl.BlockSpec((1, 1, 2 * ROW_TILE), lambda i: (i, 0, 0), memory_space=pltpu.SMEM),
                  pl.BlockSpec((1, 1, 2 * ROW_TILE), lambda i: (jnp.minimum(i + 1, nt - 1), 0, 0),
                               memory_space=pltpu.SMEM),
                  pl.BlockSpec((ROW_TILE, LANES), lambda i: (i, 0)), row,
                  pl.BlockSpec(memory_space=pl.ANY)],
        out_specs=row,
        out_shape=jax.ShapeDtypeStruct((t, D_MODEL), F32),
        scratch_shapes=[pltpu.VMEM((2, 2, ROW_TILE) + ROW_AS_TILE, F32), pltpu.SemaphoreType.DMA((2,))],
        compiler_params=_params("arbitrary"),
        name="moe_combine",
    )(dest3, dest3, route, x, ybuf)


def _moe(x, hn, route, counts, w_in, w_out, layer):
    t = x.shape[0]
    n_rb = -(-2 * t // ROW_BLOCK) + N_EXPERTS
    cap = n_rb * ROW_BLOCK
    cnt = counts[:, 0, :N_EXPERTS].astype(jnp.int32)
    total = jnp.sum(cnt, axis=0)
    padded = (total + ROW_BLOCK - 1) // ROW_BLOCK * ROW_BLOCK
    pad_end = jnp.cumsum(padded)
    pad_start = pad_end - padded
    tile_start = pad_start[None, :] + jnp.cumsum(cnt, axis=0) - cnt
    ids = route[:, 0:2].astype(jnp.int32)
    picked = ids[:, :, None] == jnp.arange(N_EXPERTS, dtype=jnp.int32)[None, None, :]
    starts = jnp.repeat(tile_start, ROW_TILE, axis=0)[:, None, :]
    dest = jnp.sum(jnp.where(picked, starts, 0), axis=-1) + route[:, 4:6].astype(jnp.int32)
    dest3 = dest.reshape(t // ROW_TILE, 1, 2 * ROW_TILE)
    seg = jnp.stack([pad_start + total, pad_end]).astype(jnp.int32)
    block_row = jnp.arange(n_rb, dtype=jnp.int32) * ROW_BLOCK
    block_expert = jnp.minimum(jnp.sum(pad_end[None, :] <= block_row[:, None], axis=1), N_EXPERTS - 1).astype(jnp.int32)
    nused = (pad_end[-1:] // ROW_BLOCK).astype(jnp.int32)
    xbuf = _dispatch(dest3, seg, nused, hn, cap)
    ybuf = _experts(block_expert, nused, xbuf, w_in, w_out, layer)
    return _combine(dest3, route, x, ybuf)


def _kvq_kernel(x_ref, kvn_ref, qn_ref, wk_ref, wvt_ref, wqt_ref, knw_ref, qnw_ref, k_ref, vt_ref, qt_ref, km_ref):
    x = x_ref[...]
    per_step = x.shape[0] // MOBA_BLOCK
    hkv = _rms(x, kvn_ref[...]).astype(BF16)
    k = _head_norm(jnp.dot(hkv, wk_ref[...], preferred_element_type=F32), knw_ref[...])
    k_ref[...] = k.astype(k_ref.dtype)
    vt = _dot_nt(wvt_ref[...], hkv)
    hq = _rms(x, qn_ref[...]).astype(BF16)
    qt = _dot_nt(wqt_ref[...], hq)
    qnw = qnw_ref[...]
    for j in range(per_step):
        toks = slice(j * MOBA_BLOCK, (j + 1) * MOBA_BLOCK)
        km_ref[j] = jnp.mean(k[toks], axis=0, keepdims=True)
        for h in range(N_HEADS):
            hs = slice(h * HEAD_DIM, (h + 1) * HEAD_DIM)
            vt_ref[0, h, j, :HEAD_DIM] = vt[hs, toks].astype(vt_ref.dtype)
            vt_ref[0, h, j, HEAD_DIM:] = jnp.ones((V_ROWS - HEAD_DIM, MOBA_BLOCK), vt_ref.dtype)
            seg = qt[hs, toks]
            qt_ref[0, h, j] = seg * lax.rsqrt(jnp.mean(seg * seg, axis=0, keepdims=True) + EPS) * qnw


def _kvq(x, kv_norm, q_prenorm, w_kv, w_q, k_norm, q_norm, batch, nb):
    t = x.shape[0]
    per_step = KV_TILE // MOBA_BLOCK
    assert nb % per_step == 0
    steps = nb // per_step
    row = pl.BlockSpec((KV_TILE, D_MODEL), lambda i: (i, 0))

    def tblk(rows):
        return pl.BlockSpec((1, N_HEADS, per_step, rows, MOBA_BLOCK), lambda i: (i // steps, 0, i % steps, 0, 0))

    def tshape(rows):
        return (batch, N_HEADS, nb, rows, MOBA_BLOCK)

    w_k = w_kv[:, :D_MODEL].astype(BF16)
    w_vt = w_kv[:, D_MODEL:].T.astype(BF16)
    w_qt = w_q.T.astype(BF16)
    qnw = jnp.broadcast_to(q_norm.reshape(HEAD_DIM, 1), (HEAD_DIM, MOBA_BLOCK))
    wspec = _const_spec((D_MODEL, D_MODEL))
    return pl.pallas_call(
        _kvq_kernel,
        grid=(t // KV_TILE,),
        in_specs=[row, _const_spec((1, D_MODEL)), _const_spec((1, D_MODEL)), wspec, wspec, wspec,
                  _const_spec((1, HEAD_DIM)), _const_spec((HEAD_DIM, MOBA_BLOCK))],
        out_specs=[row, tblk(V_ROWS), tblk(HEAD_DIM), pl.BlockSpec((per_step, 1, D_MODEL), lambda i: (i, 0, 0))],
        out_shape=[jax.ShapeDtypeStruct((t, D_MODEL), BF16), jax.ShapeDtypeStruct(tshape(V_ROWS), BF16),
                   jax.ShapeDtypeStruct(tshape(HEAD_DIM), F32),
                   jax.ShapeDtypeStruct((t // MOBA_BLOCK, 1, D_MODEL), F32)],
        compiler_params=_params("parallel"),
        name="kvq",
    )(x, kv_norm, q_prenorm, w_k, w_vt, w_qt, k_norm, qnw)


def _bias_tiles_kernel(rb_ref, o_ref):
    h, delta = pl.program_id(0), pl.program_id(1)
    shape = (MOBA_BLOCK, MOBA_BLOCK)
    dist = delta * MOBA_BLOCK + lax.broadcasted_iota(jnp.int32, shape, 1) - lax.broadcasted_iota(jnp.int32, shape, 0)
    n = jnp.maximum(dist, 0)
    max_exact = NUM_BUCKETS // 2
    log_ratio = (jnp.log(jnp.maximum(n, max_exact).astype(F32) / max_exact)
                 / math.log(MAX_DISTANCE / max_exact))
    large = max_exact + (log_ratio * (NUM_BUCKETS - max_exact)).astype(jnp.int32)
    bucket = jnp.where(n < max_exact, n, jnp.minimum(large, NUM_BUCKETS - 1))
    bias = jnp.zeros(shape, F32)
    for b in range(NUM_BUCKETS):
        bias = jnp.where(bucket == b, rb_ref[b, h], bias)
    o_ref[0, 0] = jnp.where(dist >= 0, bias * LOG2E, NEG)


def _bias_tiles(rel_bias):
    return pl.pallas_call(
        _bias_tiles_kernel,
        grid=(N_HEADS, N_NEAR + 1),
        in_specs=[pl.BlockSpec(memory_space=pltpu.SMEM)],
        out_specs=pl.BlockSpec((1, 1, MOBA_BLOCK, MOBA_BLOCK), lambda h, d: (h, d, 0, 0)),
        out_shape=jax.ShapeDtypeStruct((N_HEADS, N_NEAR + 1, MOBA_BLOCK, MOBA_BLOCK), F32),
        compiler_params=_params("parallel", "parallel"),
        name="bias_tiles",
    )(rel_bias)


def _select_kernel(qt_ref, km_ref, o_ref):
    i = pl.program_id(1)
    nb = km_ref.shape[1]
    km = km_ref[0]
    blk = lax.broadcasted_iota(jnp.int32, (nb, MOBA_BLOCK), 0)
    blkf = blk.astype(F32)
    for h in range(N_HEADS):
        gate = jnp.dot(km[:, h * HEAD_DIM:(h + 1) * HEAD_DIM], qt_ref[0, h, 0],
                       precision=HIGHEST, preferred_element_type=F32)
        g = jnp.where(blk < i, gate, NEG)
        mask = jnp.full((nb, MOBA_BLOCK), NEG, F32)
        for _ in range(MOBA_TOPK):
            m = jnp.max(g, axis=0, keepdims=True)
            idx = jnp.min(jnp.where(g == m, blkf, float(nb)), axis=0, keepdims=True)
            hit = (blkf == idx) & (m > 0.5 * NEG)
            mask = jnp.where(hit, 0.0, mask)
            g = jnp.where(blkf == idx, NEG, g)
        o_ref[0, h, 0] = jnp.where(blk == i, 0.0, mask)


def _select(qt, kmean, batch, nb):
    tblk = pl.BlockSpec((1, N_HEADS, 1, HEAD_DIM, MOBA_BLOCK), lambda b, i: (b, 0, i, 0, 0))
    return pl.pallas_call(
        _select_kernel,
        grid=(batch, nb),
        in_specs=[tblk, pl.BlockSpec((1, nb, D_MODEL), lambda b, i: (b, 0, 0))],
        out_specs=pl.BlockSpec((1, N_HEADS, 1, nb, MOBA_BLOCK), lambda b, i: (b, 0, i, 0, 0)),
        out_shape=jax.ShapeDtypeStruct((batch, N_HEADS, nb, nb, MOBA_BLOCK), F32),
        compiler_params=_params("parallel", "parallel"),
        name="moba_select",
    )(qt, kmean)


def _attention_kernel(qt_ref, k_ref, vt_ref, mask_ref, bias_ref, o_ref, acc_ref, ss_ref, ps_ref):
    i = pl.program_id(2)
    blk = MOBA_BLOCK
    nbatch = qt_ref.shape[0]
    qts = [(qt_ref[b, 0, 0] * (ATT_SCALE * LOG2E)).astype(BF16) for b in range(nbatch)]
    far_bias = bias_ref[0, N_NEAR, 0:1, :]

    def block_of(t):
        near = jnp.where(t < ATT_NEAR_STEPS, i - t, t - ATT_NEAR_STEPS)
        return jnp.where((t == 0) | (t > i), i, near)

    def stage_s(t, slot):
        j = block_of(t)
        for b in range(nbatch):
            ss_ref[slot, b] = jnp.dot(k_ref[b, pl.ds(pl.multiple_of(j * blk, blk), blk), :], qts[b],
                                      preferred_element_type=F32)

    def stage_f(t, slot, ms, near):
        j = block_of(t)
        dead = jnp.where(t > i, NEG, 0.0)
        if near:
            bias = bias_ref[0, jnp.minimum(i - j, N_NEAR)]
        new_m, alphas = [], []
        for b in range(nbatch):
            s = ss_ref[slot, b]
            rowb = mask_ref[b, 0, 0, pl.ds(j, 1), :] + dead
            if near:
                s = s + bias
            else:
                rowb = rowb + far_bias
            mc = jnp.maximum(ms[b], jnp.max(s, axis=0, keepdims=True) + rowb)
            ps_ref[slot, b] = jnp.exp2(s - (mc - rowb)).astype(BF16)
            alphas.append(jnp.exp2(ms[b] - mc))
            new_m.append(mc)
        return tuple(new_m), tuple(alphas)

    def stage_a(t, slot, alphas):
        j = block_of(t)
        pvs = [jnp.dot(vt_ref[b, 0, j], ps_ref[slot, b], preferred_element_type=F32) for b in range(nbatch)]
        for b in range(nbatch):
            acc_ref[b] = alphas[b] * acc_ref[b] + pvs[b]

    stage_s(0, 0)
    ps_ref[1] = jnp.zeros(ps_ref.shape[1:], BF16)
    acc_ref[...] = jnp.zeros_like(acc_ref)
    row = (1, blk)
    init = (tuple(jnp.full(row, NEG, F32) for _ in range(nbatch)), tuple(jnp.ones(row, F32) for _ in range(nbatch)))

    def two_steps(tt, carry, near):
        ms, alphas = carry
        t = 2 * tt
        stage_s(t + 1, 1)
        ms, alphas0 = stage_f(t, 0, ms, near)
        stage_a(jnp.maximum(t - 1, 0), 1, alphas)
        stage_s(t + 2, 0)
        ms, alphas1 = stage_f(t + 1, 1, ms, near)
        stage_a(t, 0, alphas0)
        return ms, alphas1

    trips = (i + 2) // 2
    near_trips = jnp.minimum(trips, ATT_NEAR_STEPS // 2)
    carry = lax.fori_loop(0, near_trips, functools.partial(two_steps, near=True), init)
    ms, alphas = lax.fori_loop(near_trips, trips, functools.partial(two_steps, near=False), carry)
    stage_a(2 * trips - 1, 1, alphas)
    for b in range(nbatch):
        acc = acc_ref[b]
        o_ref[b] = (acc[:HEAD_DIM] / acc[HEAD_DIM:HEAD_DIM + 1]).T.astype(o_ref.dtype)


def _attention(qt, kn, vt, mask, bias, batch, seq):
    nb = seq // MOBA_BLOCK
    bg = max(c for c in range(1, ATT_BATCH_GROUP + 1) if batch % c == 0)
    once = pl.Buffered(2)
    tile = (2, bg, MOBA_BLOCK, MOBA_BLOCK)
    out = pl.pallas_call(
        _attention_kernel,
        grid=(batch // bg, N_HEADS, nb),
        in_specs=[pl.BlockSpec((bg, 1, 1, HEAD_DIM, MOBA_BLOCK), lambda g, h, i: (g, h, i, 0, 0)),
                  pl.BlockSpec((bg, seq, HEAD_DIM), lambda g, h, i: (g, 0, h), pipeline_mode=once),
                  pl.BlockSpec((bg, 1, nb, V_ROWS, MOBA_BLOCK), lambda g, h, i: (g, h, 0, 0, 0), pipeline_mode=once),
                  pl.BlockSpec((bg, 1, 1, nb, MOBA_BLOCK), lambda g, h, i: (g, h, i, 0, 0)),
                  pl.BlockSpec((1, N_NEAR + 1, MOBA_BLOCK, MOBA_BLOCK), lambda g, h, i: (h, 0, 0, 0))],
        out_specs=pl.BlockSpec((bg, MOBA_BLOCK, HEAD_DIM), lambda g, h, i: (g, i, h)),
        out_shape=jax.ShapeDtypeStruct((batch, seq, D_MODEL), BF16),
        scratch_shapes=[pltpu.VMEM((bg, V_ROWS, MOBA_BLOCK), F32), pltpu.VMEM(tile, F32), pltpu.VMEM(tile, BF16)],
        compiler_params=_params("parallel", "parallel", "arbitrary"),
        name="moba_attention",
    )(qt, kn.reshape(batch, seq, D_MODEL), vt, mask, bias)
    return out.reshape(batch * seq, D_MODEL)


def kernel(x, mix_norm, ffn_norm, hg_w_in, hg_lb, hg_o_norm, hg_w_out, kv_norm, w_kv, k_norm, att_w_q, q_norm, att_w_o, rel_bias, moe_w_group, moe_b_group, moe_w_expert, moe_b_expert, moe_w_in, moe_w_out):
    batch, seq, d = x.shape
    assert d == D_MODEL and seq % GLA_TILE == 0 and seq % MOBA_BLOCK == 0
    assert mix_norm.shape[0] == 2 and hg_w_in.shape[0] == 1 and att_w_q.shape[0] == 1
    x0 = x.reshape(batch * seq, d)

    q, f, v, g = _hg_proj(x0, mix_norm[0:1], hg_lb, hg_w_in[0].astype(BF16))
    og = _gla(q, f, v, g, hg_o_norm[0:1], batch, seq)
    x1, hn, route, counts = _proj_route(
        og, x0, hg_w_out[0].astype(BF16), ffn_norm[0:1],
        *_router_weights(moe_w_group[0], moe_b_group[0], moe_w_expert[0], moe_b_expert[0]))
    x2 = _moe(x1, hn, route, counts, moe_w_in, moe_w_out, 0)

    nb = seq // MOBA_BLOCK
    kn, vt, qt, kmean = _kvq(x2, kv_norm[None, :], mix_norm[1:2], w_kv, att_w_q[0], k_norm[None, :], q_norm[0],
                             batch, nb)
    mask = _select(qt, kmean.reshape(batch, nb, d), batch, nb)
    att = _attention(qt, kn, vt, mask, _bias_tiles(rel_bias), batch, seq)
    x3, hn, route, counts = _proj_route(
        att, x2, att_w_o[0].astype(BF16), ffn_norm[1:2],
        *_router_weights(moe_w_group[1], moe_b_group[1], moe_w_expert[1], moe_b_expert[1]))
    out = _moe(x3, hn, route, counts, moe_w_in, moe_w_out, 1)
    return out.reshape(batch, seq, d)
```

```python
import functools
import math

import jax
import jax.numpy as jnp
import numpy as np
from jax import lax
from jax.experimental import pallas as pl
from jax.experimental.pallas import tpu as pltpu

F32 = jnp.float32
BF16 = jnp.bfloat16

D_MODEL = 1024
N_HEADS = 8
HEAD_DIM = 128
HG_CHUNK = 64
ATT_SCALE = HEAD_DIM ** -0.5
MOBA_BLOCK = 256
MOBA_TOPK = 3
NUM_BUCKETS = 32
MAX_DISTANCE = 1024
N_GROUPS = 4
EXPERTS_PER_GROUP = 8
N_EXPERTS = N_GROUPS * EXPERTS_PER_GROUP
EXPERT_FF = D_MODEL // 2
EPS = 1e-6
NEG = -1e30
LOG2E = math.log2(math.e)
ATT_BATCH_GROUP = 4

LANES = 128
SUBLANES = 8
ROW_AS_TILE = (SUBLANES, LANES)
assert SUBLANES * LANES == D_MODEL
ROW_TILE = 512
GLA_TILE = 512
KV_TILE = 512
ROW_BLOCK = 512
ISSUE_UNROLL = 8
DISPATCH_SLOTS = 3
VMEM_LIMIT = 56 * 1024 * 1024


def _near_block_count():
    max_exact = NUM_BUCKETS // 2
    delta = 1
    while True:
        dist = delta * MOBA_BLOCK - (MOBA_BLOCK - 1)
        steps = (math.log(max(dist, max_exact) / max_exact) / math.log(MAX_DISTANCE / max_exact)
                 * (NUM_BUCKETS - max_exact))
        if steps >= NUM_BUCKETS - 1 - max_exact + 0.5:
            return delta
        delta += 1


N_NEAR = _near_block_count()
ATT_NEAR_STEPS = N_NEAR + 1 + (N_NEAR + 1) % 2
V_ROWS = HEAD_DIM + 16


def _params(*sem):
    return pltpu.CompilerParams(dimension_semantics=sem, vmem_limit_bytes=VMEM_LIMIT)


def _const_spec(shape):
    nd = len(shape)
    return pl.BlockSpec(shape, lambda *_: (0,) * nd, pipeline_mode=pl.Buffered(1))


def _rms(x, w):
    return x * lax.rsqrt(jnp.mean(x * x, axis=-1, keepdims=True) + EPS) * w


def _sigmoid(x):
    return 1.0 / (1.0 + jnp.exp(-x))


def _rows_to_tiles(value):
    cols = jnp.stack([value[:, c * LANES:(c + 1) * LANES] for c in range(SUBLANES)])
    return pltpu.einshape("crl->rcl", cols)


def _tiles_to_cols(tiles):
    return pltpu.einshape("rcl->crl", tiles)


def _dot_nt(a, b):
    return lax.dot_general(a, b, (((1,), (1,)), ((), ())), preferred_element_type=F32)


def _head_norm(x, w):
    segs = [_rms(x[:, h * HEAD_DIM:(h + 1) * HEAD_DIM], w) for h in range(N_HEADS)]
    return jnp.concatenate(segs, axis=-1)


def _hg_proj_kernel(x_ref, nw_ref, lb_ref, w_ref, q_ref, f_ref, v_ref, g_ref):
    h = _rms(x_ref[...], nw_ref[...]).astype(BF16)
    lbl = lb_ref[...]
    e = jnp.exp(lbl - jnp.max(lbl, axis=0, keepdims=True))
    lb = e[0:1] / jnp.sum(e, axis=0, keepdims=True)
    d = D_MODEL
    q = jnp.dot(h, w_ref[:, 0:d], preferred_element_type=F32)
    q_ref[...] = q * (HEAD_DIM ** -0.5)
    fl = jnp.dot(h, w_ref[:, d:2 * d], preferred_element_type=F32)
    f_ref[...] = lb + (1.0 - lb) * _sigmoid(fl)
    v_ref[...] = jnp.dot(h, w_ref[:, 2 * d:3 * d], preferred_element_type=F32)
    g = jnp.dot(h, w_ref[:, 3 * d:4 * d], preferred_element_type=F32)
    g_ref[...] = g * _sigmoid(g)


def _hg_proj(x, nw, hg_lb, w_in):
    t = x.shape[0]
    out = jax.ShapeDtypeStruct((t, D_MODEL), F32)
    row = pl.BlockSpec((ROW_TILE, D_MODEL), lambda i: (i, 0))
    return pl.pallas_call(
        _hg_proj_kernel,
        grid=(t // ROW_TILE,),
        in_specs=[row, _const_spec((1, D_MODEL)), _const_spec(hg_lb.shape), _const_spec(w_in.shape)],
        out_specs=[row, row, row, row],
        out_shape=[out, out, out, out],
        compiler_params=_params("parallel"),
        name="hg_proj",
    )(x, nw, hg_lb, w_in)


def _gla_level_tables():
    c = HG_CHUNK
    t = np.arange(c)[:, None]
    s = np.arange(c)[None, :]
    cum, mask, refs = [s <= t], [s == t], []
    h = c // 2
    while h >= 1:
        mid = t // (2 * h) * (2 * h) + h
        if h >= SUBLANES:
            refs.append(("rows", [int(mid[r, 0]) - 1 for r in range(0, c, SUBLANES)]))
        else:
            refs.append(("cum", len(cum)))
            cum.append(s <= mid - 1)
        mask.append((t // (2 * h) == s // (2 * h)) & (t % (2 * h) >= h) & (s % (2 * h) < h))
        h //= 2
    cum = np.concatenate(cum, 0).astype(np.float32)
    return np.tile(cum, (1, 3)), np.stack(mask).astype(np.float32), refs


def _gla_kernel(level_refs, q_ref, f_ref, v_ref, g_ref, nw_ref, cum_ref, msk_ref, o_ref, st_ref):
    @pl.when(pl.program_id(1) == 0)
    def _():
        st_ref[...] = jnp.zeros_like(st_ref)

    c = HG_CHUNK
    nw = nw_ref[...]
    heads = [slice(h * HEAD_DIM, (h + 1) * HEAD_DIM) for h in range(N_HEADS)]

    def chunk(ci, carry):
        rs = pl.ds(pl.multiple_of(ci * c, c), c)
        cum = cum_ref[...]
        qs, ks, vs, bbs, scs = [], [], [], [], []
        for hs in heads:
            f = f_ref[rs, hs]
            lf = jnp.log(f) * LOG2E
            l1 = lf.astype(BF16)
            r1 = lf - l1.astype(F32)
            l2 = r1.astype(BF16)
            l3 = (r1 - l2.astype(F32)).astype(BF16)
            bbs.append(jnp.dot(cum, jnp.concatenate([l1, l2, l3], axis=0), preferred_element_type=F32))
            qs.append(q_ref[rs, hs])
            ks.append(1.0 - f)
            vs.append(v_ref[rs, hs].astype(BF16))
        for h in range(N_HEADS):
            q, k, bb = qs[h], ks[h], bbs[h]
            b = bb[0:c]
            a = _dot_nt(q.astype(BF16), k.astype(BF16)) * msk_ref[0]
            for lev, (kind, where) in enumerate(level_refs, start=1):
                if kind == "rows":
                    bref = jnp.concatenate([jnp.broadcast_to(b[r:r + 1], (SUBLANES, HEAD_DIM)) for r in where], axis=0)
                else:
                    bref = bb[where * c:(where + 1) * c]
                e = jnp.exp2(-jnp.abs(b - bref))
                a = a + _dot_nt((q * e).astype(BF16), (k * e).astype(BF16)) * msk_ref[lev]
            scs.append(a.astype(BF16))
        outs = []
        for h in range(N_HEADS):
            q, k, v, b = qs[h], ks[h], vs[h], bbs[h][0:c]
            st = st_ref[h]
            o = jnp.dot(scs[h], v, preferred_element_type=F32)
            o = o + _dot_nt((q * jnp.exp2(b)).astype(BF16), st.astype(BF16))
            bl = b[c - 1:c]
            kd = (k * jnp.exp2(bl - b)).astype(BF16)
            upd = lax.dot_general(v, kd, (((0,), (0,)), ((), ())), preferred_element_type=F32)
            st_ref[h] = st * jnp.exp2(bl) + upd
            outs.append(o)
        for h, hs in enumerate(heads):
            o_ref[rs, hs] = (_rms(outs[h], nw) * g_ref[rs, hs]).astype(o_ref.dtype)
        return carry

    lax.fori_loop(0, GLA_TILE // c, chunk, 0, unroll=2)


def _gla(q, f, v, g, nw, batch, seq):
    t = q.shape[0]
    nl = seq // GLA_TILE
    cum, msk, level_refs = _gla_level_tables()
    blk = pl.BlockSpec((GLA_TILE, D_MODEL), lambda b, l: (b * nl + l, 0))
    return pl.pallas_call(
        functools.partial(_gla_kernel, level_refs),
        grid=(batch, nl),
        in_specs=[blk, blk, blk, blk, _const_spec((1, HEAD_DIM)), _const_spec(cum.shape), _const_spec(msk.shape)],
        out_specs=blk,
        out_shape=jax.ShapeDtypeStruct((t, D_MODEL), BF16),
        scratch_shapes=[pltpu.VMEM((N_HEADS, HEAD_DIM, HEAD_DIM), F32)],
        compiler_params=_params("parallel", "arbitrary"),
        name="gla",
    )(q, f, v, g, nw, jnp.asarray(cum, BF16), jnp.asarray(msk, F32))


def _proj_route_kernel(a_ref, x_ref, w_ref, nw_ref, wrh_ref, wrl_ref, br_ref, xo_ref, hn_ref, rt_ref, cnt_ref):
    y = x_ref[...] + jnp.dot(a_ref[...], w_ref[...], preferred_element_type=F32)
    xo_ref[...] = y
    hn = _rms(y, nw_ref[...])
    hn_ref[...] = _rows_to_tiles(hn)
    hh = hn.astype(BF16)
    hl = (hn - hh.astype(F32)).astype(BF16)
    logits = (jnp.dot(hh, wrh_ref[...], preferred_element_type=F32)
              + jnp.dot(hl, wrh_ref[...], preferred_element_type=F32)
              + jnp.dot(hh, wrl_ref[...], preferred_element_type=F32)) + br_ref[...]
    lane = lax.broadcasted_iota(jnp.int32, logits.shape, 1)
    lanef = lane.astype(F32)
    big = float(LANES)
    gl = jnp.where(lane < N_GROUPS, logits, NEG)
    gmax = jnp.max(gl, axis=-1, keepdims=True)
    gidx = jnp.min(jnp.where(gl == gmax, lanef, big), axis=-1, keepdims=True)
    g_w = 1.0 / jnp.sum(jnp.exp(gl - gmax), axis=-1, keepdims=True)
    lo = N_GROUPS + gidx * EXPERTS_PER_GROUP
    el = jnp.where((lanef >= lo) & (lanef < lo + EXPERTS_PER_GROUP), logits, NEG)
    m1 = jnp.max(el, axis=-1, keepdims=True)
    i1 = jnp.min(jnp.where(el == m1, lanef, big), axis=-1, keepdims=True)
    el2 = jnp.where(lanef == i1, NEG, el)
    m2 = jnp.max(el2, axis=-1, keepdims=True)
    i2 = jnp.min(jnp.where(el2 == m2, lanef, big), axis=-1, keepdims=True)
    e2 = jnp.exp(m2 - m1)
    w1 = g_w / (1.0 + e2)
    w2 = g_w * e2 / (1.0 + e2)
    ex1, ex2 = i1 - N_GROUPS, i2 - N_GROUPS
    oh1 = jnp.where(lanef == ex1, 1.0, 0.0)
    oh2 = jnp.where(lanef == ex2, 1.0, 0.0)
    n = logits.shape[0]
    before = jnp.where(lax.broadcasted_iota(jnp.int32, (n, n), 0) > lax.broadcasted_iota(jnp.int32, (n, n), 1),
                       1.0, 0.0).astype(BF16)
    c1 = jnp.dot(before, oh1.astype(BF16), preferred_element_type=F32)
    c2 = jnp.dot(before, oh2.astype(BF16), preferred_element_type=F32)
    tot1 = jnp.sum(oh1, axis=0, keepdims=True)
    r1 = jnp.sum(oh1 * c1, axis=-1, keepdims=True)
    r2 = jnp.sum(oh2 * (c2 + tot1), axis=-1, keepdims=True)
    cnt_ref[0] = tot1 + jnp.sum(oh2, axis=0, keepdims=True)
    rt = jnp.where(lane == 0, ex1, 0.0)
    rt = jnp.where(lane == 1, ex2, rt)
    rt = jnp.where(lane == 2, w1, rt)
    rt = jnp.where(lane == 3, w2, rt)
    rt = jnp.where(lane == 4, r1, rt)
    rt_ref[...] = jnp.where(lane == 5, r2, rt)


def _proj_route(a, x, w, nw, wr_hi, wr_lo, br):
    t = x.shape[0]
    row = pl.BlockSpec((ROW_TILE, D_MODEL), lambda i: (i, 0))
    rt = pl.BlockSpec((ROW_TILE, LANES), lambda i: (i, 0))
    act = jax.ShapeDtypeStruct((t, D_MODEL), F32)
    return pl.pallas_call(
        _proj_route_kernel,
        grid=(t // ROW_TILE,),
        in_specs=[row, row, _const_spec(w.shape), _const_spec((1, D_MODEL)),
                  _const_spec(wr_hi.shape), _const_spec(wr_lo.shape), _const_spec((1, LANES))],
        out_specs=[row, pl.BlockSpec((ROW_TILE,) + ROW_AS_TILE, lambda i: (i, 0, 0)), rt,
                   pl.BlockSpec((1, 1, LANES), lambda i: (i, 0, 0))],
        out_shape=[act, jax.ShapeDtypeStruct((t,) + ROW_AS_TILE, F32), jax.ShapeDtypeStruct((t, LANES), F32),
                   jax.ShapeDtypeStruct((t // ROW_TILE, 1, LANES), F32)],
        compiler_params=_params("parallel"),
        name="proj_route",
    )(a, x, w, nw, wr_hi, wr_lo, br)


def _router_weights(w_group, b_group, w_expert, b_expert):
    pad = LANES - N_GROUPS - N_EXPERTS
    w = jnp.concatenate([w_group, w_expert, jnp.zeros((D_MODEL, pad), F32)], axis=1)
    b = jnp.concatenate([b_group, b_expert, jnp.zeros((pad,), F32)])[None, :]
    hi = w.astype(BF16)
    lo = (w - hi.astype(F32)).astype(BF16)
    return hi, lo, b


def _dispatch_kernel(dest_ref, seg_ref, nused_ref, hn_hbm, xbuf_hbm, stage, zero_ref, lsem, ssem, zsem):
    i = pl.program_id(0)
    last = pl.num_programs(0) - 1
    slot = i % DISPATCH_SLOTS

    def load(step):
        s = step % DISPATCH_SLOTS
        return pltpu.make_async_copy(hn_hbm.at[pl.ds(pl.multiple_of(step * ROW_TILE, ROW_TILE), ROW_TILE)],
                                     stage.at[s], lsem.at[s])

    def wait_scatter(step):
        s = step % DISPATCH_SLOTS
        for k in range(2):
            pltpu.make_async_copy(stage.at[s], xbuf_hbm.at[pl.ds(0, ROW_TILE)], ssem.at[s]).wait()

    def zero_row(r):
        return pltpu.make_async_copy(zero_ref.at[pl.ds(0, 1)], xbuf_hbm.at[pl.ds(r, 1)], zsem)

    def zero_block(b):
        return pltpu.make_async_copy(zero_ref, xbuf_hbm.at[pl.ds(pl.multiple_of(b * ROW_BLOCK, ROW_BLOCK), ROW_BLOCK)], zsem)

    def for_each_padding_row(fn):
        def per_expert(e, carry):
            def row(r, c):
                fn(r)
                return c
            return lax.fori_loop(seg_ref[0, e], seg_ref[1, e], row, carry)
        lax.fori_loop(0, N_EXPERTS, per_expert, 0)

    def for_each_unused_block(fn):
        def block(b, c):
            fn(b)
            return c
        lax.fori_loop(nused_ref[0], xbuf_hbm.shape[0] // ROW_BLOCK, block, 0)

    @pl.when(i == 0)
    def _():
        load(0).start()
        zero_ref[...] = jnp.zeros_like(zero_ref)
        for_each_padding_row(lambda r: zero_row(r).start())
        for_each_unused_block(lambda b: zero_block(b).start())
        for_each_padding_row(lambda r: zero_row(r).wait())
        for_each_unused_block(lambda b: zero_block(b).wait())

    load(i).wait()

    @pl.when(i < last)
    def _():
        load(i + 1).start()

    def issue(r, carry):
        for k in range(2):
            pltpu.make_async_copy(stage.at[slot, pl.ds(r, 1)], xbuf_hbm.at[pl.ds(dest_ref[0, 0, 2 * r + k], 1)],
                                  ssem.at[slot]).start(priority=k)
        return carry

    lax.fori_loop(0, ROW_TILE, issue, 0, unroll=ISSUE_UNROLL)

    @pl.when(i > 0)
    def _():
        wait_scatter(i - 1)

    @pl.when(i == last)
    def _():
        wait_scatter(i)


def _dispatch(dest3, seg, nused, hn, cap):
    t = hn.shape[0]
    ring = (DISPATCH_SLOTS,)
    return pl.pallas_call(
        _dispatch_kernel,
        grid=(t // ROW_TILE,),
        in_specs=[pl.BlockSpec((1, 1, 2 * ROW_TILE), lambda i: (i, 0, 0), memory_space=pltpu.SMEM),
                  pl.BlockSpec(memory_space=pltpu.SMEM), pl.BlockSpec(memory_space=pltpu.SMEM),
                  pl.BlockSpec(memory_space=pl.ANY)],
        out_specs=pl.BlockSpec(memory_space=pl.ANY),
        out_shape=jax.ShapeDtypeStruct((cap,) + ROW_AS_TILE, F32),
        scratch_shapes=[pltpu.VMEM(ring + (ROW_TILE,) + ROW_AS_TILE, F32), pltpu.VMEM((ROW_BLOCK,) + ROW_AS_TILE, F32),
                        pltpu.SemaphoreType.DMA(ring), pltpu.SemaphoreType.DMA(ring), pltpu.SemaphoreType.DMA],
        compiler_params=pltpu.CompilerParams(dimension_semantics=("arbitrary",), has_side_effects=True),
        name="moe_dispatch",
    )(dest3, seg, nused, hn)


def _experts_kernel(be_ref, nused_ref, x_ref, wi_ref, wo_ref, y_ref, wi_bf, wo_bf):
    i = pl.program_id(0)
    used = i < nused_ref[0]
    new_expert = (i == 0) | (be_ref[i] != be_ref[jnp.maximum(i - 1, 0)])

    @pl.when(used & new_expert)
    def _():
        wi_bf[...] = wi_ref[0, 0].astype(BF16)
        wo_bf[...] = wo_ref[0, 0].astype(BF16)

    @pl.when(used)
    def _():
        cols = _tiles_to_cols(x_ref[...])
        x = jnp.concatenate([cols[c] for c in range(SUBLANES)], axis=-1)
        hu = jnp.dot(x.astype(BF16), wi_bf[...], preferred_element_type=F32)
        u = hu[:, :EXPERT_FF]
        act = u * _sigmoid(u) * hu[:, EXPERT_FF:]
        y_ref[...] = _rows_to_tiles(jnp.dot(act.astype(BF16), wo_bf[...], preferred_element_type=F32))

    @pl.when(jnp.logical_not(used))
    def _():
        y_ref[...] = jnp.zeros_like(y_ref)


def _experts(block_expert, nused, xbuf, w_in, w_out, layer):
    cap = xbuf.shape[0]
    grid_spec = pltpu.PrefetchScalarGridSpec(
        num_scalar_prefetch=2,
        grid=(cap // ROW_BLOCK,),
        in_specs=[pl.BlockSpec((ROW_BLOCK,) + ROW_AS_TILE, lambda i, be, nu: (jnp.minimum(i, nu[0] - 1), 0, 0)),
                  pl.BlockSpec((1, 1, D_MODEL, 2 * EXPERT_FF), lambda i, be, nu: (layer, be[i], 0, 0)),
                  pl.BlockSpec((1, 1, EXPERT_FF, D_MODEL), lambda i, be, nu: (layer, be[i], 0, 0))],
        out_specs=pl.BlockSpec((ROW_BLOCK,) + ROW_AS_TILE, lambda i, be, nu: (i, 0, 0)),
        scratch_shapes=[pltpu.VMEM((D_MODEL, 2 * EXPERT_FF), BF16), pltpu.VMEM((EXPERT_FF, D_MODEL), BF16)],
    )
    return pl.pallas_call(
        _experts_kernel,
        grid_spec=grid_spec,
        out_shape=jax.ShapeDtypeStruct((cap,) + ROW_AS_TILE, F32),
        compiler_params=_params("arbitrary"),
        name="moe_experts",
    )(block_expert, nused, xbuf, w_in, w_out)


def _combine_kernel(dest_ref, dest_next_ref, rt_ref, x_ref, y_hbm, o_ref, buf, sem):
    i = pl.program_id(0)
    slot = i % 2

    def gather(dref, s):
        def issue(r, carry):
            for k in range(2):
                pltpu.make_async_copy(y_hbm.at[pl.ds(dref[0, 0, 2 * r + k], 1)], buf.at[s, k, pl.ds(r, 1)],
                                      sem.at[s]).start(priority=k)
            return carry
        lax.fori_loop(0, ROW_TILE, issue, 0, unroll=ISSUE_UNROLL)

    @pl.when(i == 0)
    def _():
        gather(dest_ref, 0)

    @pl.when(i + 1 < pl.num_programs(0))
    def _():
        gather(dest_next_ref, 1 - slot)

    for k in range(2):
        pltpu.make_async_copy(y_hbm.at[pl.ds(0, ROW_TILE)], buf.at[slot, k], sem.at[slot]).wait()
    rt = rt_ref[...]
    w1, w2 = rt[:, 2:3], rt[:, 3:4]
    g1 = _tiles_to_cols(buf[slot, 0])
    g2 = _tiles_to_cols(buf[slot, 1])
    for c in range(SUBLANES):
        cols = slice(c * LANES, (c + 1) * LANES)
        o_ref[:, cols] = x_ref[:, cols] + w1 * g1[c] + w2 * g2[c]


def _combine(dest3, route, x, ybuf):
    t = x.shape[0]
    nt = t // ROW_TILE
    row = pl.BlockSpec((ROW_TILE, D_MODEL), lambda i: (i, 0))
    return pl.pallas_call(
        _combine_kernel,
        grid=(nt,),
        in_specs=[pl.BlockSpec((1, 1, 2 * ROW_TILE), lambda i: (i, 0, 0), memory_space=pltpu.SMEM),
                  pl.BlockSpec((1, 1, 2 * ROW_TILE), lambda i: (jnp.minimum(i + 1, nt - 1), 0, 0),
                               memory_space=pltpu.SMEM),
                  pl.BlockSpec((ROW_TILE, LANES), lambda i: (i, 0)), row,
                  pl.BlockSpec(memory_space=pl.ANY)],
        out_specs=row,
        out_shape=jax.ShapeDtypeStruct((t, D_MODEL), F32),
        scratch_shapes=[pltpu.VMEM((2, 2, ROW_TILE) + ROW_AS_TILE, F32), pltpu.SemaphoreType.DMA((2,))],
        compiler_params=_params("arbitrary"),
        name="moe_combine",
    )(dest3, dest3, route, x, ybuf)


def _moe(x, hn, route, counts, w_in, w_out, layer):
    t = x.shape[0]
    n_rb = -(-2 * t // ROW_BLOCK) + N_EXPERTS
    cap = n_rb * ROW_BLOCK
    cnt = counts[:, 0, :N_EXPERTS].astype(jnp.int32)
    total = jnp.sum(cnt, axis=0)
    padded = (total + ROW_BLOCK - 1) // ROW_BLOCK * ROW_BLOCK
    pad_end = jnp.cumsum(padded)
    pad_start = pad_end - padded
    tile_start = pad_start[None, :] + jnp.cumsum(cnt, axis=0) - cnt
    ids = route[:, 0:2].astype(jnp.int32)
    picked = ids[:, :, None] == jnp.arange(N_EXPERTS, dtype=jnp.int32)[None, None, :]
    starts = jnp.repeat(tile_start, ROW_TILE, axis=0)[:, None, :]
    dest = jnp.sum(jnp.where(picked, starts, 0), axis=-1) + route[:, 4:6].astype(jnp.int32)
    dest3 = dest.reshape(t // ROW_TILE, 1, 2 * ROW_TILE)
    seg = jnp.stack([pad_start + total, pad_end]).astype(jnp.int32)
    block_row = jnp.arange(n_rb, dtype=jnp.int32) * ROW_BLOCK
    block_expert = jnp.minimum(jnp.sum(pad_end[None, :] <= block_row[:, None], axis=1), N_EXPERTS - 1).astype(jnp.int32)
    nused = (pad_end[-1:] // ROW_BLOCK).astype(jnp.int32)
    xbuf = _dispatch(dest3, seg, nused, hn, cap)
    ybuf = _experts(block_expert, nused, xbuf, w_in, w_out, layer)
    return _combine(dest3, route, x, ybuf)


def _kvq_kernel(x_ref, kvn_ref, qn_ref, wk_ref, wvt_ref, wqt_ref, knw_ref, qnw_ref, k_ref, vt_ref, qt_ref, km_ref):
    x = x_ref[...]
    per_step = x.shape[0] // MOBA_BLOCK
    hkv = _rms(x, kvn_ref[...]).astype(BF16)
    k = _head_norm(jnp.dot(hkv, wk_ref[...], preferred_element_type=F32), knw_ref[...])
    k_ref[...] = k.astype(k_ref.dtype)
    vt = _dot_nt(wvt_ref[...], hkv)
    hq = _rms(x, qn_ref[...]).astype(BF16)
    qt = _dot_nt(wqt_ref[...], hq)
    qnw = qnw_ref[...]
    for j in range(per_step):
        toks = slice(j * MOBA_BLOCK, (j + 1) * MOBA_BLOCK)
        km_ref[j] = jnp.mean(k[toks], axis=0, keepdims=True)
        for h in range(N_HEADS):
            hs = slice(h * HEAD_DIM, (h + 1) * HEAD_DIM)
            vt_ref[0, h, j, :HEAD_DIM] = vt[hs, toks].astype(vt_ref.dtype)
            vt_ref[0, h, j, HEAD_DIM:] = jnp.ones((V_ROWS - HEAD_DIM, MOBA_BLOCK), vt_ref.dtype)
            seg = qt[hs, toks]
            qt_ref[0, h, j] = seg * lax.rsqrt(jnp.mean(seg * seg, axis=0, keepdims=True) + EPS) * qnw


def _kvq(x, kv_norm, q_prenorm, w_kv, w_q, k_norm, q_norm, batch, nb):
    t = x.shape[0]
    per_step = KV_TILE // MOBA_BLOCK
    assert nb % per_step == 0
    steps = nb // per_step
    row = pl.BlockSpec((KV_TILE, D_MODEL), lambda i: (i, 0))

    def tblk(rows):
        return pl.BlockSpec((1, N_HEADS, per_step, rows, MOBA_BLOCK), lambda i: (i // steps, 0, i % steps, 0, 0))

    def tshape(rows):
        return (batch, N_HEADS, nb, rows, MOBA_BLOCK)

    w_k = w_kv[:, :D_MODEL].astype(BF16)
    w_vt = w_kv[:, D_MODEL:].T.astype(BF16)
    w_qt = w_q.T.astype(BF16)
    qnw = jnp.broadcast_to(q_norm.reshape(HEAD_DIM, 1), (HEAD_DIM, MOBA_BLOCK))
    wspec = _const_spec((D_MODEL, D_MODEL))
    return pl.pallas_call(
        _kvq_kernel,
        grid=(t // KV_TILE,),
        in_specs=[row, _const_spec((1, D_MODEL)), _const_spec((1, D_MODEL)), wspec, wspec, wspec,
                  _const_spec((1, HEAD_DIM)), _const_spec((HEAD_DIM, MOBA_BLOCK))],
        out_specs=[row, tblk(V_ROWS), tblk(HEAD_DIM), pl.BlockSpec((per_step, 1, D_MODEL), lambda i: (i, 0, 0))],
        out_shape=[jax.ShapeDtypeStruct((t, D_MODEL), BF16), jax.ShapeDtypeStruct(tshape(V_ROWS), BF16),
                   jax.ShapeDtypeStruct(tshape(HEAD_DIM), F32),
                   jax.ShapeDtypeStruct((t // MOBA_BLOCK, 1, D_MODEL), F32)],
        compiler_params=_params("parallel"),
        name="kvq",
    )(x, kv_norm, q_prenorm, w_k, w_vt, w_qt, k_norm, qnw)


def _bias_tiles_kernel(rb_ref, o_ref):
    h, delta = pl.program_id(0), pl.program_id(1)
    shape = (MOBA_BLOCK, MOBA_BLOCK)
    dist = delta * MOBA_BLOCK + lax.broadcasted_iota(jnp.int32, shape, 1) - lax.broadcasted_iota(jnp.int32, shape, 0)
    n = jnp.maximum(dist, 0)
    max_exact = NUM_BUCKETS // 2
    log_ratio = (jnp.log(jnp.maximum(n, max_exact).astype(F32) / max_exact)
                 / math.log(MAX_DISTANCE / max_exact))
    large = max_exact + (log_ratio * (NUM_BUCKETS - max_exact)).astype(jnp.int32)
    bucket = jnp.where(n < max_exact, n, jnp.minimum(large, NUM_BUCKETS - 1))
    bias = jnp.zeros(shape, F32)
    for b in range(NUM_BUCKETS):
        bias = jnp.where(bucket == b, rb_ref[b, h], bias)
    o_ref[0, 0] = jnp.where(dist >= 0, bias * LOG2E, NEG)


def _bias_tiles(rel_bias):
    return pl.pallas_call(
        _bias_tiles_kernel,
        grid=(N_HEADS, N_NEAR + 1),
        in_specs=[pl.BlockSpec(memory_space=pltpu.SMEM)],
        out_specs=pl.BlockSpec((1, 1, MOBA_BLOCK, MOBA_BLOCK), lambda h, d: (h, d, 0, 0)),
        out_shape=jax.ShapeDtypeStruct((N_HEADS, N_NEAR + 1, MOBA_BLOCK, MOBA_BLOCK), F32),
        compiler_params=_params("parallel", "parallel"),
        name="bias_tiles",
    )(rel_bias)


def _select_kernel(qt_ref, km_ref, o_ref):
    i = pl.program_id(1)
    nb = km_ref.shape[1]
    km = km_ref[0]
    blk = lax.broadcasted_iota(jnp.int32, (nb, MOBA_BLOCK), 0)
    blkf = blk.astype(F32)
    for h in range(N_HEADS):
        kmh, qth = km[:, h * HEAD_DIM:(h + 1) * HEAD_DIM], qt_ref[0, h, 0]
        k_hi, q_hi = kmh.astype(BF16), qth.astype(BF16)
        k_lo, q_lo = (kmh - k_hi.astype(F32)).astype(BF16), (qth - q_hi.astype(F32)).astype(BF16)
        gate = (jnp.dot(k_hi, q_hi, preferred_element_type=F32) + jnp.dot(k_lo, q_hi, preferred_element_type=F32)
                + jnp.dot(k_hi, q_lo, preferred_element_type=F32))
        g = jnp.where(blk < i, gate, NEG)
        mask = jnp.full((nb, MOBA_BLOCK), NEG, F32)
        for _ in range(MOBA_TOPK):
            m = jnp.max(g, axis=0, keepdims=True)
            idx = jnp.min(jnp.where(g == m, blkf, float(nb)), axis=0, keepdims=True)
            hit = (blkf == idx) & (m > 0.5 * NEG)
            mask = jnp.where(hit, 0.0, mask)
            g = jnp.where(blkf == idx, NEG, g)
        o_ref[0, h, 0] = jnp.where(blk == i, 0.0, mask)


def _select(qt, kmean, batch, nb):
    tblk = pl.BlockSpec((1, N_HEADS, 1, HEAD_DIM, MOBA_BLOCK), lambda b, i: (b, 0, i, 0, 0))
    return pl.pallas_call(
        _select_kernel,
        grid=(batch, nb),
        in_specs=[tblk, pl.BlockSpec((1, nb, D_MODEL), lambda b, i: (b, 0, 0))],
        out_specs=pl.BlockSpec((1, N_HEADS, 1, nb, MOBA_BLOCK), lambda b, i: (b, 0, i, 0, 0)),
        out_shape=jax.ShapeDtypeStruct((batch, N_HEADS, nb, nb, MOBA_BLOCK), F32),
        compiler_params=_params("parallel", "parallel"),
        name="moba_select",
    )(qt, kmean)


def _attention_kernel(qt_ref, k_ref, vt_ref, mask_ref, bias_ref, o_ref, acc_ref, ss0_ref, ss1_ref, ps0_ref, ps1_ref):
    i = pl.program_id(2)
    blk = MOBA_BLOCK
    nbatch = qt_ref.shape[0]
    ss_refs, ps_refs = (ss0_ref, ss1_ref), (ps0_ref, ps1_ref)
    qts = [(qt_ref[b, 0, 0] * (ATT_SCALE * LOG2E)).astype(BF16) for b in range(nbatch)]
    far_bias = bias_ref[0, N_NEAR, 0:1, :]

    def block_of(t):
        near = jnp.where(t < ATT_NEAR_STEPS, i - t, t - ATT_NEAR_STEPS)
        return jnp.where((t == 0) | (t > i), i, near)

    def stage_s(t, slot):
        j = block_of(t)
        for b in range(nbatch):
            ss_refs[slot][b] = jnp.dot(k_ref[b, pl.ds(pl.multiple_of(j * blk, blk), blk), :], qts[b],
                                      preferred_element_type=F32)

    def stage_f(t, slot, ms, near):
        j = block_of(t)
        dead = jnp.where(t > i, NEG, 0.0)
        if near:
            bias = bias_ref[0, jnp.minimum(i - j, N_NEAR)]
        new_m, alphas = [], []
        for b in range(nbatch):
            s = ss_refs[slot][b]
            rowb = mask_ref[b, 0, 0, pl.ds(j, 1), :] + dead
            if near:
                s = s + bias
            else:
                rowb = rowb + far_bias
            mc = jnp.maximum(ms[b], jnp.max(s, axis=0, keepdims=True) + rowb)
            ps_refs[slot][b] = jnp.exp2(s - (mc - rowb)).astype(BF16)
            alphas.append(jnp.exp2(ms[b] - mc))
            new_m.append(mc)
        return tuple(new_m), tuple(alphas)

    def stage_a(t, slot, alphas):
        j = block_of(t)
        pvs = [jnp.dot(vt_ref[b, 0, j], ps_refs[slot][b], preferred_element_type=F32) for b in range(nbatch)]
        for b in range(nbatch):
            acc_ref[b] = alphas[b] * acc_ref[b] + pvs[b]

    stage_s(0, 0)
    ps1_ref[...] = jnp.zeros_like(ps1_ref)
    acc_ref[...] = jnp.zeros_like(acc_ref)
    row = (1, blk)
    init = (tuple(jnp.full(row, NEG, F32) for _ in range(nbatch)), tuple(jnp.ones(row, F32) for _ in range(nbatch)))

    def two_steps(tt, carry, near):
        ms, alphas = carry
        t = 2 * tt
        stage_s(t + 1, 1)
        ms, alphas0 = stage_f(t, 0, ms, near)
        stage_a(jnp.maximum(t - 1, 0), 1, alphas)
        stage_s(t + 2, 0)
        ms, alphas1 = stage_f(t + 1, 1, ms, near)
        stage_a(t, 0, alphas0)
        return ms, alphas1

    trips = (i + 2) // 2
    near_trips = jnp.minimum(trips, ATT_NEAR_STEPS // 2)
    carry = lax.fori_loop(0, near_trips, functools.partial(two_steps, near=True), init)
    ms, alphas = lax.fori_loop(near_trips, trips, functools.partial(two_steps, near=False), carry)
    stage_a(2 * trips - 1, 1, alphas)
    for b in range(nbatch):
        acc = acc_ref[b]
        o_ref[b] = (acc[:HEAD_DIM] / acc[HEAD_DIM:HEAD_DIM + 1]).T.astype(o_ref.dtype)


def _attention(qt, kn, vt, mask, bias, batch, seq):
    nb = seq // MOBA_BLOCK
    bg = max(c for c in range(1, ATT_BATCH_GROUP + 1) if batch % c == 0)
    once = pl.Buffered(2)
    tile = (bg, MOBA_BLOCK, MOBA_BLOCK)
    out = pl.pallas_call(
        _attention_kernel,
        grid=(batch // bg, N_HEADS, nb),
        in_specs=[pl.BlockSpec((bg, 1, 1, HEAD_DIM, MOBA_BLOCK), lambda g, h, i: (g, h, i, 0, 0)),
                  pl.BlockSpec((bg, seq, HEAD_DIM), lambda g, h, i: (g, 0, h), pipeline_mode=once),
                  pl.BlockSpec((bg, 1, nb, V_ROWS, MOBA_BLOCK), lambda g, h, i: (g, h, 0, 0, 0), pipeline_mode=once),
                  pl.BlockSpec((bg, 1, 1, nb, MOBA_BLOCK), lambda g, h, i: (g, h, i, 0, 0)),
                  pl.BlockSpec((1, N_NEAR + 1, MOBA_BLOCK, MOBA_BLOCK), lambda g, h, i: (h, 0, 0, 0))],
        out_specs=pl.BlockSpec((bg, MOBA_BLOCK, HEAD_DIM), lambda g, h, i: (g, i, h)),
        out_shape=jax.ShapeDtypeStruct((batch, seq, D_MODEL), BF16),
        scratch_shapes=[pltpu.VMEM((bg, V_ROWS, MOBA_BLOCK), F32), pltpu.VMEM(tile, F32), pltpu.VMEM(tile, F32), pltpu.VMEM(tile, BF16), pltpu.VMEM(tile, BF16)],
        compiler_params=_params("parallel", "parallel", "arbitrary"),
        name="moba_attention",
    )(qt, kn.reshape(batch, seq, D_MODEL), vt, mask, bias)
    return out.reshape(batch * seq, D_MODEL)


def kernel(x, mix_norm, ffn_norm, hg_w_in, hg_lb, hg_o_norm, hg_w_out, kv_norm, w_kv, k_norm, att_w_q, q_norm, att_w_o, rel_bias, moe_w_group, moe_b_group, moe_w_expert, moe_b_expert, moe_w_in, moe_w_out):
    batch, seq, d = x.shape
    assert d == D_MODEL and seq % GLA_TILE == 0 and seq % MOBA_BLOCK == 0
    assert mix_norm.shape[0] == 2 and hg_w_in.shape[0] == 1 and att_w_q.shape[0] == 1
    x0 = x.reshape(batch * seq, d)

    q, f, v, g = _hg_proj(x0, mix_norm[0:1], hg_lb, hg_w_in[0].astype(BF16))
    og = _gla(q, f, v, g, hg_o_norm[0:1], batch, seq)
    x1, hn, route, counts = _proj_route(
        og, x0, hg_w_out[0].astype(BF16), ffn_norm[0:1],
        *_router_weights(moe_w_group[0], moe_b_group[0], moe_w_expert[0], moe_b_expert[0]))
    x2 = _moe(x1, hn, route, counts, moe_w_in, moe_w_out, 0)

    nb = seq // MOBA_BLOCK
    kn, vt, qt, kmean = _kvq(x2, kv_norm[None, :], mix_norm[1:2], w_kv, att_w_q[0], k_norm[None, :], q_norm[0],
                             batch, nb)
    mask = _select(qt, kmean.reshape(batch, nb, d), batch, nb)
    att = _attention(qt, kn, vt, mask, _bias_tiles(rel_bias), batch, seq)
    x3, hn, route, counts = _proj_route(
        att, x2, att_w_o[0].astype(BF16), ffn_norm[1:2],
        *_router_weights(moe_w_group[1], moe_b_group[1], moe_w_expert[1], moe_b_expert[1]))
    out = _moe(x3, hn, route, counts, moe_w_in, moe_w_out, 1)
    return out.reshape(batch, seq, d)
```

```python
import functools
import math

import jax
import jax.numpy as jnp
import numpy as np
from jax import lax
from jax.experimental import pallas as pl
from jax.experimental.pallas import tpu as pltpu

F32 = jnp.float32
BF16 = jnp.bfloat16

D_MODEL = 1024
N_HEADS = 8
HEAD_DIM = 128
HG_CHUNK = 64
ATT_SCALE = HEAD_DIM ** -0.5
MOBA_BLOCK = 256
MOBA_TOPK = 3
NUM_BUCKETS = 32
MAX_DISTANCE = 1024
N_GROUPS = 4
EXPERTS_PER_GROUP = 8
N_EXPERTS = N_GROUPS * EXPERTS_PER_GROUP
EXPERT_FF = D_MODEL // 2
EPS = 1e-6
NEG = -1e30
LOG2E = math.log2(math.e)
ATT_BATCH_GROUP = 4

LANES = 128
SUBLANES = 8
ROW_AS_TILE = (SUBLANES, LANES)
assert SUBLANES * LANES == D_MODEL
ROW_TILE = 512
GLA_TILE = 512
GLA_UNROLL = 4
KV_TILE = 512
ROW_BLOCK = 512
ISSUE_UNROLL = 8
DISPATCH_SLOTS = 3
VMEM_LIMIT = 56 * 1024 * 1024


def _near_block_count():
    max_exact = NUM_BUCKETS // 2
    delta = 1
    while True:
        dist = delta * MOBA_BLOCK - (MOBA_BLOCK - 1)
        steps = (math.log(max(dist, max_exact) / max_exact) / math.log(MAX_DISTANCE / max_exact)
                 * (NUM_BUCKETS - max_exact))
        if steps >= NUM_BUCKETS - 1 - max_exact + 0.5:
            return delta
        delta += 1


N_NEAR = _near_block_count()
ATT_NEAR_STEPS = N_NEAR + 1 + (N_NEAR + 1) % 2
V_ROWS = HEAD_DIM + 16


def _params(*sem):
    return pltpu.CompilerParams(dimension_semantics=sem, vmem_limit_bytes=VMEM_LIMIT)


def _const_spec(shape):
    nd = len(shape)
    return pl.BlockSpec(shape, lambda *_: (0,) * nd, pipeline_mode=pl.Buffered(1))


def _rms(x, w):
    return x * lax.rsqrt(jnp.mean(x * x, axis=-1, keepdims=True) + EPS) * w


def _sigmoid(x):
    return 1.0 / (1.0 + jnp.exp(-x))


def _rows_to_tiles(value):
    cols = jnp.stack([value[:, c * LANES:(c + 1) * LANES] for c in range(SUBLANES)])
    return pltpu.einshape("crl->rcl", cols)


def _tiles_to_cols(tiles):
    return pltpu.einshape("rcl->crl", tiles)


def _dot_nt(a, b):
    return lax.dot_general(a, b, (((1,), (1,)), ((), ())), preferred_element_type=F32)


def _head_norm(x, w):
    segs = [_rms(x[:, h * HEAD_DIM:(h + 1) * HEAD_DIM], w) for h in range(N_HEADS)]
    return jnp.concatenate(segs, axis=-1)


def _hg_proj_kernel(x_ref, nw_ref, lb_ref, w_ref, q_ref, f_ref, v_ref, g_ref):
    h = _rms(x_ref[...], nw_ref[...]).astype(BF16)
    lbl = lb_ref[...]
    e = jnp.exp(lbl - jnp.max(lbl, axis=0, keepdims=True))
    lb = e[0:1] / jnp.sum(e, axis=0, keepdims=True)
    d = D_MODEL
    q = jnp.dot(h, w_ref[:, 0:d], preferred_element_type=F32)
    q_ref[...] = q * (HEAD_DIM ** -0.5)
    fl = jnp.dot(h, w_ref[:, d:2 * d], preferred_element_type=F32)
    f_ref[...] = lb + (1.0 - lb) * _sigmoid(fl)
    v_ref[...] = jnp.dot(h, w_ref[:, 2 * d:3 * d], preferred_element_type=F32)
    g = jnp.dot(h, w_ref[:, 3 * d:4 * d], preferred_element_type=F32)
    g_ref[...] = g * _sigmoid(g)


def _hg_proj(x, nw, hg_lb, w_in):
    t = x.shape[0]
    out = jax.ShapeDtypeStruct((t, D_MODEL), F32)
    row = pl.BlockSpec((ROW_TILE, D_MODEL), lambda i: (i, 0))
    return pl.pallas_call(
        _hg_proj_kernel,
        grid=(t // ROW_TILE,),
        in_specs=[row, _const_spec((1, D_MODEL)), _const_spec(hg_lb.shape), _const_spec(w_in.shape)],
        out_specs=[row, row, row, row],
        out_shape=[out, out, out, out],
        compiler_params=_params("parallel"),
        name="hg_proj",
    )(x, nw, hg_lb, w_in)


def _gla_level_tables():
    c = HG_CHUNK
    t = np.arange(c)[:, None]
    s = np.arange(c)[None, :]
    cum, mask, refs = [s <= t], [s == t], []
    h = c // 2
    while h >= 1:
        mid = t // (2 * h) * (2 * h) + h
        if h >= SUBLANES:
            refs.append(("rows", [int(mid[r, 0]) - 1 for r in range(0, c, SUBLANES)]))
        else:
            refs.append(("cum", len(cum)))
            cum.append(s <= mid - 1)
        mask.append((t // (2 * h) == s // (2 * h)) & (t % (2 * h) >= h) & (s % (2 * h) < h))
        h //= 2
    cum = np.concatenate(cum, 0).astype(np.float32)
    return np.tile(cum, (1, 3)), np.stack(mask).astype(np.float32), refs


def _gla_kernel(level_refs, q_ref, f_ref, v_ref, g_ref, nw_ref, cum_ref, msk_ref, o_ref, st_ref):
    @pl.when(pl.program_id(1) == 0)
    def _():
        st_ref[...] = jnp.zeros_like(st_ref)

    c = HG_CHUNK
    nw = nw_ref[...]
    heads = [slice(h * HEAD_DIM, (h + 1) * HEAD_DIM) for h in range(N_HEADS)]

    def chunk(ci, carry):
        rs = pl.ds(pl.multiple_of(ci * c, c), c)
        cum = cum_ref[...]
        qs, ks, vs, bbs, scs = [], [], [], [], []
        for hs in heads:
            f = f_ref[rs, hs]
            lf = jnp.log(f) * LOG2E
            l1 = lf.astype(BF16)
            r1 = lf - l1.astype(F32)
            l2 = r1.astype(BF16)
            l3 = (r1 - l2.astype(F32)).astype(BF16)
            bbs.append(jnp.dot(cum, jnp.concatenate([l1, l2, l3], axis=0), preferred_element_type=F32))
            qs.append(q_ref[rs, hs])
            ks.append(1.0 - f)
            vs.append(v_ref[rs, hs].astype(BF16))
        for h in range(N_HEADS):
            q, k, bb = qs[h], ks[h], bbs[h]
            b = bb[0:c]
            a = _dot_nt(q.astype(BF16), k.astype(BF16)) * msk_ref[0]
            for lev, (kind, where) in enumerate(level_refs, start=1):
                if kind == "rows":
                    bref = jnp.concatenate([jnp.broadcast_to(b[r:r + 1], (SUBLANES, HEAD_DIM)) for r in where], axis=0)
                else:
                    bref = bb[where * c:(where + 1) * c]
                e = jnp.exp2(-jnp.abs(b - bref))
                a = a + _dot_nt((q * e).astype(BF16), (k * e).astype(BF16)) * msk_ref[lev]
            scs.append(a.astype(BF16))
        outs = []
        for h in range(N_HEADS):
            q, k, v, b = qs[h], ks[h], vs[h], bbs[h][0:c]
            st = st_ref[h]
            o = jnp.dot(scs[h], v, preferred_element_type=F32)
            o = o + _dot_nt((q * jnp.exp2(b)).astype(BF16), st.astype(BF16))
            bl = b[c - 1:c]
            kd = (k * jnp.exp2(bl - b)).astype(BF16)
            upd = lax.dot_general(v, kd, (((0,), (0,)), ((), ())), preferred_element_type=F32)
            st_ref[h] = st * jnp.exp2(bl) + upd
            outs.append(o)
        for h, hs in enumerate(heads):
            o_ref[rs, hs] = (_rms(outs[h], nw) * g_ref[rs, hs]).astype(o_ref.dtype)
        return carry

    lax.fori_loop(0, GLA_TILE // c, chunk, 0, unroll=GLA_UNROLL)


def _gla(q, f, v, g, nw, batch, seq):
    t = q.shape[0]
    nl = seq // GLA_TILE
    cum, msk, level_refs = _gla_level_tables()
    blk = pl.BlockSpec((GLA_TILE, D_MODEL), lambda b, l: (b * nl + l, 0))
    return pl.pallas_call(
        functools.partial(_gla_kernel, level_refs),
        grid=(batch, nl),
        in_specs=[blk, blk, blk, blk, _const_spec((1, HEAD_DIM)), _const_spec(cum.shape), _const_spec(msk.shape)],
        out_specs=blk,
        out_shape=jax.ShapeDtypeStruct((t, D_MODEL), BF16),
        scratch_shapes=[pltpu.VMEM((N_HEADS, HEAD_DIM, HEAD_DIM), F32)],
        compiler_params=_params("parallel", "arbitrary"),
        name="gla",
    )(q, f, v, g, nw, jnp.asarray(cum, BF16), jnp.asarray(msk, F32))


def _proj_route_kernel(a_ref, x_ref, w_ref, nw_ref, wrh_ref, wrl_ref, br_ref, xo_ref, hn_ref, rt_ref, rtt_ref, cnt_ref):
    y = x_ref[...] + jnp.dot(a_ref[...], w_ref[...], preferred_element_type=F32)
    xo_ref[...] = y
    hn = _rms(y, nw_ref[...])
    hn_ref[...] = _rows_to_tiles(hn)
    hh = hn.astype(BF16)
    hl = (hn - hh.astype(F32)).astype(BF16)
    both = jnp.dot(hh, jnp.concatenate([wrh_ref[...], wrl_ref[...]], axis=1), preferred_element_type=F32)
    logits = (both[:, :LANES] + both[:, LANES:]
              + jnp.dot(hl, wrh_ref[...], preferred_element_type=F32)) + br_ref[...]
    lane = lax.broadcasted_iota(jnp.int32, logits.shape, 1)
    lanef = lane.astype(F32)
    big = float(LANES)
    gl = jnp.where(lane < N_GROUPS, logits, NEG)
    gmax = jnp.max(gl, axis=-1, keepdims=True)
    gidx = jnp.min(jnp.where(gl == gmax, lanef, big), axis=-1, keepdims=True)
    g_w = 1.0 / jnp.sum(jnp.exp(gl - gmax), axis=-1, keepdims=True)
    lo = N_GROUPS + gidx * EXPERTS_PER_GROUP
    el = jnp.where((lanef >= lo) & (lanef < lo + EXPERTS_PER_GROUP), logits, NEG)
    m1 = jnp.max(el, axis=-1, keepdims=True)
    i1 = jnp.min(jnp.where(el == m1, lanef, big), axis=-1, keepdims=True)
    el2 = jnp.where(lanef == i1, NEG, el)
    m2 = jnp.max(el2, axis=-1, keepdims=True)
    i2 = jnp.min(jnp.where(el2 == m2, lanef, big), axis=-1, keepdims=True)
    e2 = jnp.exp(m2 - m1)
    w1 = g_w / (1.0 + e2)
    w2 = g_w * e2 / (1.0 + e2)
    ex1, ex2 = i1 - N_GROUPS, i2 - N_GROUPS
    oh1 = jnp.where(lanef == ex1, 1.0, 0.0)
    oh2 = jnp.where(lanef == ex2, 1.0, 0.0)
    n = logits.shape[0]
    before = jnp.where(lax.broadcasted_iota(jnp.int32, (n, n), 0) > lax.broadcasted_iota(jnp.int32, (n, n), 1),
                       1.0, 0.0).astype(BF16)
    c1 = jnp.dot(before, oh1.astype(BF16), preferred_element_type=F32)
    c2 = jnp.dot(before, oh2.astype(BF16), preferred_element_type=F32)
    tot1 = jnp.sum(oh1, axis=0, keepdims=True)
    r1 = jnp.sum(oh1 * c1, axis=-1, keepdims=True)
    r2 = jnp.sum(oh2 * (c2 + tot1), axis=-1, keepdims=True)
    cnt_ref[0] = tot1 + jnp.sum(oh2, axis=0, keepdims=True)
    rt = jnp.where(lane == 0, ex1, 0.0)
    rt = jnp.where(lane == 1, ex2, rt)
    rt = jnp.where(lane == 2, w1, rt)
    rt = jnp.where(lane == 3, w2, rt)
    rt = jnp.where(lane == 4, r1, rt)
    rt = jnp.where(lane == 5, r2, rt)
    rt_ref[...] = rt
    rtt_ref[...] = rt.T[:SUBLANES]


def _proj_route(a, x, w, nw, wr_hi, wr_lo, br):
    t = x.shape[0]
    row = pl.BlockSpec((ROW_TILE, D_MODEL), lambda i: (i, 0))
    rt = pl.BlockSpec((ROW_TILE, LANES), lambda i: (i, 0))
    act = jax.ShapeDtypeStruct((t, D_MODEL), F32)
    return pl.pallas_call(
        _proj_route_kernel,
        grid=(t // ROW_TILE,),
        in_specs=[row, row, _const_spec(w.shape), _const_spec((1, D_MODEL)),
                  _const_spec(wr_hi.shape), _const_spec(wr_lo.shape), _const_spec((1, LANES))],
        out_specs=[row, pl.BlockSpec((ROW_TILE,) + ROW_AS_TILE, lambda i: (i, 0, 0)), rt,
                   pl.BlockSpec((SUBLANES, ROW_TILE), lambda i: (0, i)),
                   pl.BlockSpec((1, 1, LANES), lambda i: (i, 0, 0))],
        out_shape=[act, jax.ShapeDtypeStruct((t,) + ROW_AS_TILE, F32), jax.ShapeDtypeStruct((t, LANES), F32),
                   jax.ShapeDtypeStruct((SUBLANES, t), F32),
                   jax.ShapeDtypeStruct((t // ROW_TILE, 1, LANES), F32)],
        compiler_params=_params("parallel"),
        name="proj_route",
    )(a, x, w, nw, wr_hi, wr_lo, br)


def _router_weights(w_group, b_group, w_expert, b_expert):
    pad = LANES - N_GROUPS - N_EXPERTS
    w = jnp.concatenate([w_group, w_expert, jnp.zeros((D_MODEL, pad), F32)], axis=1)
    b = jnp.concatenate([b_group, b_expert, jnp.zeros((pad,), F32)])[None, :]
    hi = w.astype(BF16)
    lo = (w - hi.astype(F32)).astype(BF16)
    return hi, lo, b


def _dispatch_kernel(dest_ref, seg_ref, nused_ref, hn_hbm, xbuf_hbm, stage, zero_ref, lsem, ssem, zsem):
    i = pl.program_id(0)
    last = pl.num_programs(0) - 1
    slot = i % DISPATCH_SLOTS

    def load(step):
        s = step % DISPATCH_SLOTS
        return pltpu.make_async_copy(hn_hbm.at[pl.ds(pl.multiple_of(step * ROW_TILE, ROW_TILE), ROW_TILE)],
                                     stage.at[s], lsem.at[s])

    def wait_scatter(step):
        s = step % DISPATCH_SLOTS
        for k in range(2):
            pltpu.make_async_copy(stage.at[s], xbuf_hbm.at[pl.ds(0, ROW_TILE)], ssem.at[s]).wait()

    def zero_row(r):
        return pltpu.make_async_copy(zero_ref.at[pl.ds(0, 1)], xbuf_hbm.at[pl.ds(r, 1)], zsem)

    def zero_block(b):
        return pltpu.make_async_copy(zero_ref, xbuf_hbm.at[pl.ds(pl.multiple_of(b * ROW_BLOCK, ROW_BLOCK), ROW_BLOCK)], zsem)

    def for_each_padding_row(fn):
        def per_expert(e, carry):
            def row(r, c):
                fn(r)
                return c
            return lax.fori_loop(seg_ref[0, e], seg_ref[1, e], row, carry)
        lax.fori_loop(0, N_EXPERTS, per_expert, 0)

    def for_each_unused_block(fn):
        def block(b, c):
            fn(b)
            return c
        lax.fori_loop(nused_ref[0], xbuf_hbm.shape[0] // ROW_BLOCK, block, 0)

    @pl.when(i == 0)
    def _():
        load(0).start()
        zero_ref[...] = jnp.zeros_like(zero_ref)
        for_each_padding_row(lambda r: zero_row(r).start())
        for_each_unused_block(lambda b: zero_block(b).start())
        for_each_padding_row(lambda r: zero_row(r).wait())
        for_each_unused_block(lambda b: zero_block(b).wait())

    load(i).wait()

    @pl.when(i < last)
    def _():
        load(i + 1).start()

    def issue(r, carry):
        for k in range(2):
            pltpu.make_async_copy(stage.at[slot, pl.ds(r, 1)], xbuf_hbm.at[pl.ds(dest_ref[0, 0, k * ROW_TILE + r], 1)],
                                  ssem.at[slot]).start(priority=k)
        return carry

    lax.fori_loop(0, ROW_TILE, issue, 0, unroll=ISSUE_UNROLL)

    @pl.when(i > 0)
    def _():
        wait_scatter(i - 1)

    @pl.when(i == last)
    def _():
        wait_scatter(i)


def _dispatch(dest3, seg, nused, hn, cap):
    t = hn.shape[0]
    ring = (DISPATCH_SLOTS,)
    return pl.pallas_call(
        _dispatch_kernel,
        grid=(t // ROW_TILE,),
        in_specs=[pl.BlockSpec((1, 1, 2 * ROW_TILE), lambda i: (i, 0, 0), memory_space=pltpu.SMEM),
                  pl.BlockSpec(memory_space=pltpu.SMEM), pl.BlockSpec(memory_space=pltpu.SMEM),
                  pl.BlockSpec(memory_space=pl.ANY)],
        out_specs=pl.BlockSpec(memory_space=pl.ANY),
        out_shape=jax.ShapeDtypeStruct((cap,) + ROW_AS_TILE, F32),
        scratch_shapes=[pltpu.VMEM(ring + (ROW_TILE,) + ROW_AS_TILE, F32), pltpu.VMEM((ROW_BLOCK,) + ROW_AS_TILE, F32),
                        pltpu.SemaphoreType.DMA(ring), pltpu.SemaphoreType.DMA(ring), pltpu.SemaphoreType.DMA],
        compiler_params=pltpu.CompilerParams(dimension_semantics=("arbitrary",), has_side_effects=True),
        name="moe_dispatch",
    )(dest3, seg, nused, hn)


def _experts_kernel(be_ref, nused_ref, x_ref, wi_ref, wo_ref, y_ref, wi_bf, wo_bf):
    i = pl.program_id(0)
    used = i < nused_ref[0]
    new_expert = (i == 0) | (be_ref[i] != be_ref[jnp.maximum(i - 1, 0)])

    @pl.when(used & new_expert)
    def _():
        wi_bf[...] = wi_ref[0, 0].astype(BF16)
        wo_bf[...] = wo_ref[0, 0].astype(BF16)

    @pl.when(used)
    def _():
        cols = _tiles_to_cols(x_ref[...])
        x = jnp.concatenate([cols[c] for c in range(SUBLANES)], axis=-1)
        hu = jnp.dot(x.astype(BF16), wi_bf[...], preferred_element_type=F32)
        u = hu[:, :EXPERT_FF]
        act = u * _sigmoid(u) * hu[:, EXPERT_FF:]
        y_ref[...] = _rows_to_tiles(jnp.dot(act.astype(BF16), wo_bf[...], preferred_element_type=F32))

    @pl.when(jnp.logical_not(used))
    def _():
        y_ref[...] = jnp.zeros_like(y_ref)


def _experts(block_expert, nused, xbuf, w_in, w_out, layer):
    cap = xbuf.shape[0]
    grid_spec = pltpu.PrefetchScalarGridSpec(
        num_scalar_prefetch=2,
        grid=(cap // ROW_BLOCK,),
        in_specs=[pl.BlockSpec((ROW_BLOCK,) + ROW_AS_TILE, lambda i, be, nu: (jnp.minimum(i, nu[0] - 1), 0, 0)),
                  pl.BlockSpec((1, 1, D_MODEL, 2 * EXPERT_FF), lambda i, be, nu: (layer, be[i], 0, 0)),
                  pl.BlockSpec((1, 1, EXPERT_FF, D_MODEL), lambda i, be, nu: (layer, be[i], 0, 0))],
        out_specs=pl.BlockSpec((ROW_BLOCK,) + ROW_AS_TILE, lambda i, be, nu: (i, 0, 0)),
        scratch_shapes=[pltpu.VMEM((D_MODEL, 2 * EXPERT_FF), BF16), pltpu.VMEM((EXPERT_FF, D_MODEL), BF16)],
    )
    return pl.pallas_call(
        _experts_kernel,
        grid_spec=grid_spec,
        out_shape=jax.ShapeDtypeStruct((cap,) + ROW_AS_TILE, F32),
        compiler_params=_params("arbitrary"),
        name="moe_experts",
    )(block_expert, nused, xbuf, w_in, w_out)


def _combine_kernel(dest_ref, dest_next_ref, rt_ref, x_ref, y_hbm, o_ref, buf, sem):
    i = pl.program_id(0)
    slot = i % 2

    def gather(dref, s):
        def issue(r, carry):
            for k in range(2):
                pltpu.make_async_copy(y_hbm.at[pl.ds(dref[0, 0, k * ROW_TILE + r], 1)], buf.at[s, k, pl.ds(r, 1)],
                                      sem.at[s]).start(priority=k)
            return carry
        lax.fori_loop(0, ROW_TILE, issue, 0, unroll=ISSUE_UNROLL)

    @pl.when(i == 0)
    def _():
        gather(dest_ref, 0)

    @pl.when(i + 1 < pl.num_programs(0))
    def _():
        gather(dest_next_ref, 1 - slot)

    for k in range(2):
        pltpu.make_async_copy(y_hbm.at[pl.ds(0, ROW_TILE)], buf.at[slot, k], sem.at[slot]).wait()
    rt = rt_ref[...]
    w1, w2 = rt[:, 2:3], rt[:, 3:4]
    g1 = _tiles_to_cols(buf[slot, 0])
    g2 = _tiles_to_cols(buf[slot, 1])
    for c in range(SUBLANES):
        cols = slice(c * LANES, (c + 1) * LANES)
        o_ref[:, cols] = x_ref[:, cols] + w1 * g1[c] + w2 * g2[c]


def _combine(dest3, route, x, ybuf):
    t = x.shape[0]
    nt = t // ROW_TILE
    row = pl.BlockSpec((ROW_TILE, D_MODEL), lambda i: (i, 0))
    return pl.pallas_call(
        _combine_kernel,
        grid=(nt,),
        in_specs=[pl.BlockSpec((1, 1, 2 * ROW_TILE), lambda i: (i, 0, 0), memory_space=pltpu.SMEM),
                  pl.BlockSpec((1, 1, 2 * ROW_TILE), lambda i: (jnp.minimum(i + 1, nt - 1), 0, 0),
                               memory_space=pltpu.SMEM),
                  pl.BlockSpec((ROW_TILE, LANES), lambda i: (i, 0)), row,
                  pl.BlockSpec(memory_space=pl.ANY)],
        out_specs=row,
        out_shape=jax.ShapeDtypeStruct((t, D_MODEL), F32),
        scratch_shapes=[pltpu.VMEM((2, 2, ROW_TILE) + ROW_AS_TILE, F32), pltpu.SemaphoreType.DMA((2,))],
        compiler_params=_params("arbitrary"),
        name="moe_combine",
    )(dest3, dest3, route, x, ybuf)


def _moe(x, hn, route, route_rows, counts, w_in, w_out, layer):
    t = x.shape[0]
    nt = t // ROW_TILE
    n_rb = -(-2 * t // ROW_BLOCK) + N_EXPERTS
    cap = n_rb * ROW_BLOCK
    cnt = counts[:, 0, :N_EXPERTS].astype(jnp.int32)
    total = jnp.sum(cnt, axis=0)
    padded = (total + ROW_BLOCK - 1) // ROW_BLOCK * ROW_BLOCK
    pad_end = jnp.cumsum(padded)
    pad_start = pad_end - padded
    tile_start = pad_start[None, :] + jnp.cumsum(cnt, axis=0) - cnt
    ids = route_rows[0:2].astype(jnp.int32)
    picked = ids[None] == jnp.arange(N_EXPERTS, dtype=jnp.int32)[:, None, None]
    starts = jnp.broadcast_to(tile_start.T[:, :, None], (N_EXPERTS, nt, ROW_TILE)).reshape(N_EXPERTS, 1, t)
    dest = jnp.sum(jnp.where(picked, starts, 0), axis=0) + route_rows[4:6].astype(jnp.int32)
    dest3 = dest.reshape(2, nt, ROW_TILE).transpose(1, 0, 2).reshape(nt, 1, 2 * ROW_TILE)
    seg = jnp.stack([pad_start + total, pad_end]).astype(jnp.int32)
    block_row = jnp.arange(n_rb, dtype=jnp.int32) * ROW_BLOCK
    block_expert = jnp.minimum(jnp.sum(pad_end[None, :] <= block_row[:, None], axis=1), N_EXPERTS - 1).astype(jnp.int32)
    nused = (pad_end[-1:] // ROW_BLOCK).astype(jnp.int32)
    xbuf = _dispatch(dest3, seg, nused, hn, cap)
    ybuf = _experts(block_expert, nused, xbuf, w_in, w_out, layer)
    return _combine(dest3, route, x, ybuf)


def _kvq_kernel(x_ref, kvn_ref, qn_ref, wk_ref, wvt_ref, wqt_ref, knw_ref, qnw_ref, k_ref, vt_ref, qt_ref, km_ref):
    x = x_ref[...]
    per_step = x.shape[0] // MOBA_BLOCK
    hkv = _rms(x, kvn_ref[...]).astype(BF16)
    k = _head_norm(jnp.dot(hkv, wk_ref[...], preferred_element_type=F32), knw_ref[...])
    k_ref[...] = k.astype(k_ref.dtype)
    vt = _dot_nt(wvt_ref[...], hkv)
    hq = _rms(x, qn_ref[...]).astype(BF16)
    qt = _dot_nt(wqt_ref[...], hq)
    qnw = qnw_ref[...]
    for j in range(per_step):
        toks = slice(j * MOBA_BLOCK, (j + 1) * MOBA_BLOCK)
        km_ref[j] = jnp.mean(k[toks], axis=0, keepdims=True)
        for h in range(N_HEADS):
            hs = slice(h * HEAD_DIM, (h + 1) * HEAD_DIM)
            vt_ref[0, h, j, :HEAD_DIM] = vt[hs, toks].astype(vt_ref.dtype)
            vt_ref[0, h, j, HEAD_DIM:] = jnp.ones((V_ROWS - HEAD_DIM, MOBA_BLOCK), vt_ref.dtype)
            seg = qt[hs, toks]
            qt_ref[0, h, j] = seg * lax.rsqrt(jnp.mean(seg * seg, axis=0, keepdims=True) + EPS) * qnw


def _kvq(x, kv_norm, q_prenorm, w_kv, w_q, k_norm, q_norm, batch, nb):
    t = x.shape[0]
    per_step = KV_TILE // MOBA_BLOCK
    assert nb % per_step == 0
    steps = nb // per_step
    row = pl.BlockSpec((KV_TILE, D_MODEL), lambda i: (i, 0))

    def tblk(rows):
        return pl.BlockSpec((1, N_HEADS, per_step, rows, MOBA_BLOCK), lambda i: (i // steps, 0, i % steps, 0, 0))

    def tshape(rows):
        return (batch, N_HEADS, nb, rows, MOBA_BLOCK)

    w_k = w_kv[:, :D_MODEL].astype(BF16)
    w_vt = w_kv[:, D_MODEL:].T.astype(BF16)
    w_qt = w_q.T.astype(BF16)
    qnw = jnp.broadcast_to(q_norm.reshape(HEAD_DIM, 1), (HEAD_DIM, MOBA_BLOCK))
    wspec = _const_spec((D_MODEL, D_MODEL))
    return pl.pallas_call(
        _kvq_kernel,
        grid=(t // KV_TILE,),
        in_specs=[row, _const_spec((1, D_MODEL)), _const_spec((1, D_MODEL)), wspec, wspec, wspec,
                  _const_spec((1, HEAD_DIM)), _const_spec((HEAD_DIM, MOBA_BLOCK))],
        out_specs=[row, tblk(V_ROWS), tblk(HEAD_DIM), pl.BlockSpec((per_step, 1, D_MODEL), lambda i: (i, 0, 0))],
        out_shape=[jax.ShapeDtypeStruct((t, D_MODEL), BF16), jax.ShapeDtypeStruct(tshape(V_ROWS), BF16),
                   jax.ShapeDtypeStruct(tshape(HEAD_DIM), F32),
                   jax.ShapeDtypeStruct((t // MOBA_BLOCK, 1, D_MODEL), F32)],
        compiler_params=_params("parallel"),
        name="kvq",
    )(x, kv_norm, q_prenorm, w_k, w_vt, w_qt, k_norm, qnw)


def _bias_tiles_kernel(rb_ref, o_ref):
    h, delta = pl.program_id(0), pl.program_id(1)
    shape = (MOBA_BLOCK, MOBA_BLOCK)
    dist = delta * MOBA_BLOCK + lax.broadcasted_iota(jnp.int32, shape, 1) - lax.broadcasted_iota(jnp.int32, shape, 0)
    n = jnp.maximum(dist, 0)
    max_exact = NUM_BUCKETS // 2
    log_ratio = (jnp.log(jnp.maximum(n, max_exact).astype(F32) / max_exact)
                 / math.log(MAX_DISTANCE / max_exact))
    large = max_exact + (log_ratio * (NUM_BUCKETS - max_exact)).astype(jnp.int32)
    bucket = jnp.where(n < max_exact, n, jnp.minimum(large, NUM_BUCKETS - 1))
    bias = jnp.zeros(shape, F32)
    for b in range(NUM_BUCKETS):
        bias = jnp.where(bucket == b, rb_ref[b, h], bias)
    o_ref[0, 0] = jnp.where(dist >= 0, bias * LOG2E, NEG)


def _bias_tiles(rel_bias):
    return pl.pallas_call(
        _bias_tiles_kernel,
        grid=(N_HEADS, N_NEAR + 1),
        in_specs=[pl.BlockSpec(memory_space=pltpu.SMEM)],
        out_specs=pl.BlockSpec((1, 1, MOBA_BLOCK, MOBA_BLOCK), lambda h, d: (h, d, 0, 0)),
        out_shape=jax.ShapeDtypeStruct((N_HEADS, N_NEAR + 1, MOBA_BLOCK, MOBA_BLOCK), F32),
        compiler_params=_params("parallel", "parallel"),
        name="bias_tiles",
    )(rel_bias)


def _select_kernel(qt_ref, km_ref, o_ref):
    i = pl.program_id(1)
    nb = km_ref.shape[1]
    km = km_ref[0]
    blk = lax.broadcasted_iota(jnp.int32, (nb, MOBA_BLOCK), 0)
    blkf = blk.astype(F32)
    for h in range(N_HEADS):
        kmh, qth = km[:, h * HEAD_DIM:(h + 1) * HEAD_DIM], qt_ref[0, h, 0]
        k_hi, q_hi = kmh.astype(BF16), qth.astype(BF16)
        k_lo, q_lo = (kmh - k_hi.astype(F32)).astype(BF16), (qth - q_hi.astype(F32)).astype(BF16)
        gate = (jnp.dot(k_hi, q_hi, preferred_element_type=F32) + jnp.dot(k_lo, q_hi, preferred_element_type=F32)
                + jnp.dot(k_hi, q_lo, preferred_element_type=F32))
        g = jnp.where(blk < i, gate, NEG)
        mask = jnp.full((nb, MOBA_BLOCK), NEG, F32)
        for _ in range(MOBA_TOPK):
            m = jnp.max(g, axis=0, keepdims=True)
            idx = jnp.min(jnp.where(g == m, blkf, float(nb)), axis=0, keepdims=True)
            hit = (blkf == idx) & (m > 0.5 * NEG)
            mask = jnp.where(hit, 0.0, mask)
            g = jnp.where(blkf == idx, NEG, g)
        o_ref[0, h, 0] = jnp.where(blk == i, 0.0, mask)


def _select(qt, kmean, batch, nb):
    tblk = pl.BlockSpec((1, N_HEADS, 1, HEAD_DIM, MOBA_BLOCK), lambda b, i: (b, 0, i, 0, 0))
    return pl.pallas_call(
        _select_kernel,
        grid=(batch, nb),
        in_specs=[tblk, pl.BlockSpec((1, nb, D_MODEL), lambda b, i: (b, 0, 0))],
        out_specs=pl.BlockSpec((1, N_HEADS, 1, nb, MOBA_BLOCK), lambda b, i: (b, 0, i, 0, 0)),
        out_shape=jax.ShapeDtypeStruct((batch, N_HEADS, nb, nb, MOBA_BLOCK), F32),
        compiler_params=_params("parallel", "parallel"),
        name="moba_select",
    )(qt, kmean)


def _attention_kernel(qt_ref, k_ref, vt_ref, mask_ref, bias_ref, o_ref, acc_ref, ss0_ref, ss1_ref, ps0_ref, ps1_ref):
    i = pl.program_id(2)
    blk = MOBA_BLOCK
    nbatch = qt_ref.shape[0]
    ss_refs, ps_refs = (ss0_ref, ss1_ref), (ps0_ref, ps1_ref)
    qts = [(qt_ref[b, 0, 0] * (ATT_SCALE * LOG2E)).astype(BF16) for b in range(nbatch)]
    far_bias = bias_ref[0, N_NEAR, 0:1, :]

    def block_of(t):
        near = jnp.where(t < ATT_NEAR_STEPS, i - t, t - ATT_NEAR_STEPS)
        return jnp.where((t == 0) | (t > i), i, near)

    def stage_s(t, slot):
        j = block_of(t)
        for b in range(nbatch):
            ss_refs[slot][b] = jnp.dot(k_ref[b, pl.ds(pl.multiple_of(j * blk, blk), blk), :], qts[b],
                                      preferred_element_type=F32)

    def stage_f(t, slot, ms, near):
        j = block_of(t)
        dead = jnp.where(t > i, NEG, 0.0)
        if near:
            bias = bias_ref[0, jnp.minimum(i - j, N_NEAR)]
        new_m, alphas = [], []
        for b in range(nbatch):
            s = ss_refs[slot][b]
            rowb = mask_ref[b, 0, 0, pl.ds(j, 1), :] + dead
            if near:
                s = s + bias
            else:
                rowb = rowb + far_bias
            mc = jnp.maximum(ms[b], jnp.max(s, axis=0, keepdims=True) + rowb)
            ps_refs[slot][b] = jnp.exp2(s - (mc - rowb)).astype(BF16)
            alphas.append(jnp.exp2(ms[b] - mc))
            new_m.append(mc)
        return tuple(new_m), tuple(alphas)

    def stage_a(t, slot, alphas):
        j = block_of(t)
        pvs = [jnp.dot(vt_ref[b, 0, j], ps_refs[slot][b], preferred_element_type=F32) for b in range(nbatch)]
        for b in range(nbatch):
            acc_ref[b] = alphas[b] * acc_ref[b] + pvs[b]

    stage_s(0, 0)
    ps1_ref[...] = jnp.zeros_like(ps1_ref)
    acc_ref[...] = jnp.zeros_like(acc_ref)
    row = (1, blk)
    init = (tuple(jnp.full(row, NEG, F32) for _ in range(nbatch)), tuple(jnp.ones(row, F32) for _ in range(nbatch)))

    def two_steps(tt, carry, near):
        ms, alphas = carry
        t = 2 * tt
        stage_s(t + 1, 1)
        ms, alphas0 = stage_f(t, 0, ms, near)
        stage_a(jnp.maximum(t - 1, 0), 1, alphas)
        stage_s(t + 2, 0)
        ms, alphas1 = stage_f(t + 1, 1, ms, near)
        stage_a(t, 0, alphas0)
        return ms, alphas1

    trips = (i + 2) // 2
    near_trips = jnp.minimum(trips, ATT_NEAR_STEPS // 2)
    carry = lax.fori_loop(0, near_trips, functools.partial(two_steps, near=True), init)
    ms, alphas = lax.fori_loop(near_trips, trips, functools.partial(two_steps, near=False), carry)
    stage_a(2 * trips - 1, 1, alphas)
    for b in range(nbatch):
        acc = acc_ref[b]
        o_ref[b] = (acc[:HEAD_DIM] / acc[HEAD_DIM:HEAD_DIM + 1]).T.astype(o_ref.dtype)


def _attention(qt, kn, vt, mask, bias, batch, seq):
    nb = seq // MOBA_BLOCK
    bg = max(c for c in range(1, ATT_BATCH_GROUP + 1) if batch % c == 0)
    once = pl.Buffered(2)
    tile = (bg, MOBA_BLOCK, MOBA_BLOCK)
    out = pl.pallas_call(
        _attention_kernel,
        grid=(batch // bg, N_HEADS, nb),
        in_specs=[pl.BlockSpec((bg, 1, 1, HEAD_DIM, MOBA_BLOCK), lambda g, h, i: (g, h, i, 0, 0)),
                  pl.BlockSpec((bg, seq, HEAD_DIM), lambda g, h, i: (g, 0, h), pipeline_mode=once),
                  pl.BlockSpec((bg, 1, nb, V_ROWS, MOBA_BLOCK), lambda g, h, i: (g, h, 0, 0, 0), pipeline_mode=once),
                  pl.BlockSpec((bg, 1, 1, nb, MOBA_BLOCK), lambda g, h, i: (g, h, i, 0, 0)),
                  pl.BlockSpec((1, N_NEAR + 1, MOBA_BLOCK, MOBA_BLOCK), lambda g, h, i: (h, 0, 0, 0))],
        out_specs=pl.BlockSpec((bg, MOBA_BLOCK, HEAD_DIM), lambda g, h, i: (g, i, h)),
        out_shape=jax.ShapeDtypeStruct((batch, seq, D_MODEL), BF16),
        scratch_shapes=[pltpu.VMEM((bg, V_ROWS, MOBA_BLOCK), F32), pltpu.VMEM(tile, F32), pltpu.VMEM(tile, F32), pltpu.VMEM(tile, BF16), pltpu.VMEM(tile, BF16)],
        compiler_params=_params("parallel", "parallel", "arbitrary"),
        name="moba_attention",
    )(qt, kn.reshape(batch, seq, D_MODEL), vt, mask, bias)
    return out.reshape(batch * seq, D_MODEL)


def kernel(x, mix_norm, ffn_norm, hg_w_in, hg_lb, hg_o_norm, hg_w_out, kv_norm, w_kv, k_norm, att_w_q, q_norm, att_w_o, rel_bias, moe_w_group, moe_b_group, moe_w_expert, moe_b_expert, moe_w_in, moe_w_out):
    batch, seq, d = x.shape
    assert d == D_MODEL and seq % GLA_TILE == 0 and seq % MOBA_BLOCK == 0
    assert mix_norm.shape[0] == 2 and hg_w_in.shape[0] == 1 and att_w_q.shape[0] == 1
    x0 = x.reshape(batch * seq, d)

    q, f, v, g = _hg_proj(x0, mix_norm[0:1], hg_lb, hg_w_in[0].astype(BF16))
    og = _gla(q, f, v, g, hg_o_norm[0:1], batch, seq)
    x1, hn, route, route_rows, counts = _proj_route(
        og, x0, hg_w_out[0].astype(BF16), ffn_norm[0:1],
        *_router_weights(moe_w_group[0], moe_b_group[0], moe_w_expert[0], moe_b_expert[0]))
    x2 = _moe(x1, hn, route, route_rows, counts, moe_w_in, moe_w_out, 0)

    nb = seq // MOBA_BLOCK
    kn, vt, qt, kmean = _kvq(x2, kv_norm[None, :], mix_norm[1:2], w_kv, att_w_q[0], k_norm[None, :], q_norm[0],
                             batch, nb)
    mask = _select(qt, kmean.reshape(batch, nb, d), batch, nb)
    att = _attention(qt, kn, vt, mask, _bias_tiles(rel_bias), batch, seq)
    x3, hn, route, route_rows, counts = _proj_route(
        att, x2, att_w_o[0].astype(BF16), ffn_norm[1:2],
        *_router_weights(moe_w_group[1], moe_b_group[1], moe_w_expert[1], moe_b_expert[1]))
    out = _moe(x3, hn, route, route_rows, counts, moe_w_in, moe_w_out, 1)
    return out.reshape(batch, seq, d)
```

```python
import functools
import math

import jax
import jax.numpy as jnp
import numpy as np
from jax import lax
from jax.experimental import pallas as pl
from jax.experimental.pallas import tpu as pltpu

F32 = jnp.float32
BF16 = jnp.bfloat16

D_MODEL = 1024
N_HEADS = 8
HEAD_DIM = 128
HG_CHUNK = 64
ATT_SCALE = HEAD_DIM ** -0.5
MOBA_BLOCK = 256
MOBA_TOPK = 3
NUM_BUCKETS = 32
MAX_DISTANCE = 1024
N_GROUPS = 4
EXPERTS_PER_GROUP = 8
N_EXPERTS = N_GROUPS * EXPERTS_PER_GROUP
EXPERT_FF = D_MODEL // 2
EPS = 1e-6
NEG = -1e30
LOG2E = math.log2(math.e)
ATT_BATCH_GROUP = 4

LANES = 128
SUBLANES = 8
ROW_AS_TILE = (SUBLANES, LANES)
assert SUBLANES * LANES == D_MODEL
ROW_TILE = 512
GLA_TILE = 512
GLA_UNROLL = 8
KV_TILE = 512
ROW_BLOCK = 512
ISSUE_UNROLL = 8
DISPATCH_SLOTS = 3
VMEM_LIMIT = 56 * 1024 * 1024


def _near_block_count():
    max_exact = NUM_BUCKETS // 2
    delta = 1
    while True:
        dist = delta * MOBA_BLOCK - (MOBA_BLOCK - 1)
        steps = (math.log(max(dist, max_exact) / max_exact) / math.log(MAX_DISTANCE / max_exact)
                 * (NUM_BUCKETS - max_exact))
        if steps >= NUM_BUCKETS - 1 - max_exact + 0.5:
            return delta
        delta += 1


N_NEAR = _near_block_count()
ATT_NEAR_STEPS = N_NEAR + 1 + (N_NEAR + 1) % 2
V_ROWS = HEAD_DIM + 16


def _params(*sem):
    return pltpu.CompilerParams(dimension_semantics=sem, vmem_limit_bytes=VMEM_LIMIT)


def _const_spec(shape):
    nd = len(shape)
    return pl.BlockSpec(shape, lambda *_: (0,) * nd, pipeline_mode=pl.Buffered(1))


def _rms(x, w):
    return x * lax.rsqrt(jnp.mean(x * x, axis=-1, keepdims=True) + EPS) * w


def _sigmoid(x):
    return 1.0 / (1.0 + jnp.exp(-x))


def _rows_to_tiles(value):
    cols = jnp.stack([value[:, c * LANES:(c + 1) * LANES] for c in range(SUBLANES)])
    return pltpu.einshape("crl->rcl", cols)


def _tiles_to_cols(tiles):
    return pltpu.einshape("rcl->crl", tiles)


def _dot_nt(a, b):
    return lax.dot_general(a, b, (((1,), (1,)), ((), ())), preferred_element_type=F32)


def _head_norm(x, w):
    segs = [_rms(x[:, h * HEAD_DIM:(h + 1) * HEAD_DIM], w) for h in range(N_HEADS)]
    return jnp.concatenate(segs, axis=-1)


def _hg_proj_kernel(x_ref, nw_ref, lb_ref, w_ref, q_ref, f_ref, v_ref, g_ref):
    h = _rms(x_ref[...], nw_ref[...]).astype(BF16)
    lbl = lb_ref[...]
    e = jnp.exp(lbl - jnp.max(lbl, axis=0, keepdims=True))
    lb = e[0:1] / jnp.sum(e, axis=0, keepdims=True)
    d = D_MODEL
    q = jnp.dot(h, w_ref[:, 0:d], preferred_element_type=F32)
    q_ref[...] = q * (HEAD_DIM ** -0.5)
    fl = jnp.dot(h, w_ref[:, d:2 * d], preferred_element_type=F32)
    f_ref[...] = lb + (1.0 - lb) * _sigmoid(fl)
    v_ref[...] = jnp.dot(h, w_ref[:, 2 * d:3 * d], preferred_element_type=F32).astype(v_ref.dtype)
    g = jnp.dot(h, w_ref[:, 3 * d:4 * d], preferred_element_type=F32)
    g_ref[...] = g * _sigmoid(g)


def _hg_proj(x, nw, hg_lb, w_in):
    t = x.shape[0]
    out = jax.ShapeDtypeStruct((t, D_MODEL), F32)
    row = pl.BlockSpec((ROW_TILE, D_MODEL), lambda i: (i, 0))
    return pl.pallas_call(
        _hg_proj_kernel,
        grid=(t // ROW_TILE,),
        in_specs=[row, _const_spec((1, D_MODEL)), _const_spec(hg_lb.shape), _const_spec(w_in.shape)],
        out_specs=[row, row, row, row],
        out_shape=[out, out, jax.ShapeDtypeStruct((t, D_MODEL), BF16), out],
        compiler_params=_params("parallel"),
        name="hg_proj",
    )(x, nw, hg_lb, w_in)


def _gla_level_tables():
    c = HG_CHUNK
    t = np.arange(c)[:, None]
    s = np.arange(c)[None, :]
    cum, mask, refs = [s <= t], [s == t], []
    h = c // 2
    while h >= 1:
        mid = t // (2 * h) * (2 * h) + h
        if h >= SUBLANES:
            refs.append(("rows", [int(mid[r, 0]) - 1 for r in range(0, c, SUBLANES)]))
        else:
            refs.append(("cum", len(cum)))
            cum.append(s <= mid - 1)
        mask.append((t // (2 * h) == s // (2 * h)) & (t % (2 * h) >= h) & (s % (2 * h) < h))
        h //= 2
    cum = np.concatenate(cum, 0).astype(np.float32)
    return np.tile(cum, (1, 3)), np.stack(mask).astype(np.float32), refs


def _gla_kernel(level_refs, q_ref, f_ref, v_ref, g_ref, nw_ref, cum_ref, msk_ref, o_ref, st_ref):
    @pl.when(pl.program_id(1) == 0)
    def _():
        st_ref[...] = jnp.zeros_like(st_ref)

    c = HG_CHUNK
    nw = nw_ref[...]
    heads = [slice(h * HEAD_DIM, (h + 1) * HEAD_DIM) for h in range(N_HEADS)]

    def chunk(ci, carry):
        rs = pl.ds(pl.multiple_of(ci * c, c), c)
        cum = cum_ref[...]
        qs, ks, vs, bbs, scs = [], [], [], [], []
        for hs in heads:
            f = f_ref[rs, hs]
            lf = jnp.log(f) * LOG2E
            l1 = lf.astype(BF16)
            r1 = lf - l1.astype(F32)
            l2 = r1.astype(BF16)
            l3 = (r1 - l2.astype(F32)).astype(BF16)
            bbs.append(jnp.dot(cum, jnp.concatenate([l1, l2, l3], axis=0), preferred_element_type=F32))
            qs.append(q_ref[rs, hs])
            ks.append(1.0 - f)
            vs.append(v_ref[rs, hs].astype(BF16))
        for h in range(N_HEADS):
            q, k, bb = qs[h], ks[h], bbs[h]
            b = bb[0:c]
            a = _dot_nt(q.astype(BF16), k.astype(BF16)) * msk_ref[0]
            for lev, (kind, where) in enumerate(level_refs, start=1):
                if kind == "rows":
                    bref = jnp.concatenate([jnp.broadcast_to(b[r:r + 1], (SUBLANES, HEAD_DIM)) for r in where], axis=0)
                else:
                    bref = bb[where * c:(where + 1) * c]
                e = jnp.exp2(-jnp.abs(b - bref))
                a = a + _dot_nt((q * e).astype(BF16), (k * e).astype(BF16)) * msk_ref[lev]
            scs.append(a.astype(BF16))
        outs = []
        for h in range(N_HEADS):
            q, k, v, b = qs[h], ks[h], vs[h], bbs[h][0:c]
            st = st_ref[h]
            o = jnp.dot(scs[h], v, preferred_element_type=F32)
            o = o + _dot_nt((q * jnp.exp2(b)).astype(BF16), st.astype(BF16))
            bl = b[c - 1:c]
            kd = (k * jnp.exp2(bl - b)).astype(BF16)
            upd = lax.dot_general(v, kd, (((0,), (0,)), ((), ())), preferred_element_type=F32)
            st_ref[h] = st * jnp.exp2(bl) + upd
            outs.append(o)
        for h, hs in enumerate(heads):
            o_ref[rs, hs] = (_rms(outs[h], nw) * g_ref[rs, hs]).astype(o_ref.dtype)
        return carry

    lax.fori_loop(0, GLA_TILE // c, chunk, 0, unroll=GLA_UNROLL)


def _gla(q, f, v, g, nw, batch, seq):
    t = q.shape[0]
    nl = seq // GLA_TILE
    cum, msk, level_refs = _gla_level_tables()
    blk = pl.BlockSpec((GLA_TILE, D_MODEL), lambda b, l: (b * nl + l, 0))
    return pl.pallas_call(
        functools.partial(_gla_kernel, level_refs),
        grid=(batch, nl),
        in_specs=[blk, blk, blk, blk, _const_spec((1, HEAD_DIM)), _const_spec(cum.shape), _const_spec(msk.shape)],
        out_specs=blk,
        out_shape=jax.ShapeDtypeStruct((t, D_MODEL), BF16),
        scratch_shapes=[pltpu.VMEM((N_HEADS, HEAD_DIM, HEAD_DIM), F32)],
        compiler_params=_params("parallel", "arbitrary"),
        name="gla",
    )(q, f, v, g, nw, jnp.asarray(cum, BF16), jnp.asarray(msk, F32))


def _proj_route_kernel(a_ref, x_ref, w_ref, nw_ref, wrh_ref, wrl_ref, br_ref, xo_ref, hn_ref, rt_ref, rtt_ref, cnt_ref):
    y = x_ref[...] + jnp.dot(a_ref[...], w_ref[...], preferred_element_type=F32)
    xo_ref[...] = y
    hn = _rms(y, nw_ref[...])
    hn_ref[...] = _rows_to_tiles(hn)
    hh = hn.astype(BF16)
    hl = (hn - hh.astype(F32)).astype(BF16)
    both = jnp.dot(hh, jnp.concatenate([wrh_ref[...], wrl_ref[...]], axis=1), preferred_element_type=F32)
    logits = (both[:, :LANES] + both[:, LANES:]
              + jnp.dot(hl, wrh_ref[...], preferred_element_type=F32)) + br_ref[...]
    lane = lax.broadcasted_iota(jnp.int32, logits.shape, 1)
    lanef = lane.astype(F32)
    big = float(LANES)
    gl = jnp.where(lane < N_GROUPS, logits, NEG)
    gmax = jnp.max(gl, axis=-1, keepdims=True)
    gidx = jnp.min(jnp.where(gl == gmax, lanef, big), axis=-1, keepdims=True)
    g_w = 1.0 / jnp.sum(jnp.exp(gl - gmax), axis=-1, keepdims=True)
    lo = N_GROUPS + gidx * EXPERTS_PER_GROUP
    el = jnp.where((lanef >= lo) & (lanef < lo + EXPERTS_PER_GROUP), logits, NEG)
    m1 = jnp.max(el, axis=-1, keepdims=True)
    i1 = jnp.min(jnp.where(el == m1, lanef, big), axis=-1, keepdims=True)
    el2 = jnp.where(lanef == i1, NEG, el)
    m2 = jnp.max(el2, axis=-1, keepdims=True)
    i2 = jnp.min(jnp.where(el2 == m2, lanef, big), axis=-1, keepdims=True)
    e2 = jnp.exp(m2 - m1)
    w1 = g_w / (1.0 + e2)
    w2 = g_w * e2 / (1.0 + e2)
    ex1, ex2 = i1 - N_GROUPS, i2 - N_GROUPS
    oh1 = jnp.where(lanef == ex1, 1.0, 0.0)
    oh2 = jnp.where(lanef == ex2, 1.0, 0.0)
    n = logits.shape[0]
    before = jnp.where(lax.broadcasted_iota(jnp.int32, (n, n), 0) > lax.broadcasted_iota(jnp.int32, (n, n), 1),
                       1.0, 0.0).astype(BF16)
    c1 = jnp.dot(before, oh1.astype(BF16), preferred_element_type=F32)
    c2 = jnp.dot(before, oh2.astype(BF16), preferred_element_type=F32)
    tot1 = jnp.sum(oh1, axis=0, keepdims=True)
    r1 = jnp.sum(oh1 * c1, axis=-1, keepdims=True)
    r2 = jnp.sum(oh2 * (c2 + tot1), axis=-1, keepdims=True)
    cnt_ref[0] = tot1 + jnp.sum(oh2, axis=0, keepdims=True)
    rt = jnp.where(lane == 0, ex1, 0.0)
    rt = jnp.where(lane == 1, ex2, rt)
    rt = jnp.where(lane == 2, w1, rt)
    rt = jnp.where(lane == 3, w2, rt)
    rt = jnp.where(lane == 4, r1, rt)
    rt = jnp.where(lane == 5, r2, rt)
    rt_ref[...] = rt
    rtt_ref[...] = rt.T[:SUBLANES]


def _proj_route(a, x, w, nw, wr_hi, wr_lo, br):
    t = x.shape[0]
    row = pl.BlockSpec((ROW_TILE, D_MODEL), lambda i: (i, 0))
    rt = pl.BlockSpec((ROW_TILE, LANES), lambda i: (i, 0))
    act = jax.ShapeDtypeStruct((t, D_MODEL), F32)
    return pl.pallas_call(
        _proj_route_kernel,
        grid=(t // ROW_TILE,),
        in_specs=[row, row, _const_spec(w.shape), _const_spec((1, D_MODEL)),
                  _const_spec(wr_hi.shape), _const_spec(wr_lo.shape), _const_spec((1, LANES))],
        out_specs=[row, pl.BlockSpec((ROW_TILE,) + ROW_AS_TILE, lambda i: (i, 0, 0)), rt,
                   pl.BlockSpec((SUBLANES, ROW_TILE), lambda i: (0, i)),
                   pl.BlockSpec((1, 1, LANES), lambda i: (i, 0, 0))],
        out_shape=[act, jax.ShapeDtypeStruct((t,) + ROW_AS_TILE, F32), jax.ShapeDtypeStruct((t, LANES), F32),
                   jax.ShapeDtypeStruct((SUBLANES, t), F32),
                   jax.ShapeDtypeStruct((t // ROW_TILE, 1, LANES), F32)],
        compiler_params=_params("parallel"),
        name="proj_route",
    )(a, x, w, nw, wr_hi, wr_lo, br)


def _router_weights(w_group, b_group, w_expert, b_expert):
    pad = LANES - N_GROUPS - N_EXPERTS
    w = jnp.concatenate([w_group, w_expert, jnp.zeros((D_MODEL, pad), F32)], axis=1)
    b = jnp.concatenate([b_group, b_expert, jnp.zeros((pad,), F32)])[None, :]
    hi = w.astype(BF16)
    lo = (w - hi.astype(F32)).astype(BF16)
    return hi, lo, b


def _dispatch_kernel(dest_ref, seg_ref, nused_ref, hn_hbm, xbuf_hbm, stage, zero_ref, lsem, ssem, zsem):
    i = pl.program_id(0)
    last = pl.num_programs(0) - 1
    slot = i % DISPATCH_SLOTS

    def load(step):
        s = step % DISPATCH_SLOTS
        return pltpu.make_async_copy(hn_hbm.at[pl.ds(pl.multiple_of(step * ROW_TILE, ROW_TILE), ROW_TILE)],
                                     stage.at[s], lsem.at[s])

    def wait_scatter(step):
        s = step % DISPATCH_SLOTS
        for k in range(2):
            pltpu.make_async_copy(stage.at[s], xbuf_hbm.at[pl.ds(0, ROW_TILE)], ssem.at[s]).wait()

    def zero_row(r):
        return pltpu.make_async_copy(zero_ref.at[pl.ds(0, 1)], xbuf_hbm.at[pl.ds(r, 1)], zsem)

    def zero_block(b):
        return pltpu.make_async_copy(zero_ref, xbuf_hbm.at[pl.ds(pl.multiple_of(b * ROW_BLOCK, ROW_BLOCK), ROW_BLOCK)], zsem)

    def for_each_padding_row(fn):
        def per_expert(e, carry):
            def row(r, c):
                fn(r)
                return c
            return lax.fori_loop(seg_ref[0, e], seg_ref[1, e], row, carry)
        lax.fori_loop(0, N_EXPERTS, per_expert, 0)

    def for_each_unused_block(fn):
        def block(b, c):
            fn(b)
            return c
        lax.fori_loop(nused_ref[0], xbuf_hbm.shape[0] // ROW_BLOCK, block, 0)

    @pl.when(i == 0)
    def _():
        load(0).start()
        zero_ref[...] = jnp.zeros_like(zero_ref)
        for_each_padding_row(lambda r: zero_row(r).start())
        for_each_unused_block(lambda b: zero_block(b).start())
        for_each_padding_row(lambda r: zero_row(r).wait())
        for_each_unused_block(lambda b: zero_block(b).wait())

    load(i).wait()

    @pl.when(i < last)
    def _():
        load(i + 1).start()

    def issue(r, carry):
        for k in range(2):
            pltpu.make_async_copy(stage.at[slot, pl.ds(r, 1)], xbuf_hbm.at[pl.ds(dest_ref[0, 0, k * ROW_TILE + r], 1)],
                                  ssem.at[slot]).start(priority=k)
        return carry

    lax.fori_loop(0, ROW_TILE, issue, 0, unroll=ISSUE_UNROLL)

    @pl.when(i > 0)
    def _():
        wait_scatter(i - 1)

    @pl.when(i == last)
    def _():
        wait_scatter(i)


def _dispatch(dest3, seg, nused, hn, cap):
    t = hn.shape[0]
    ring = (DISPATCH_SLOTS,)
    return pl.pallas_call(
        _dispatch_kernel,
        grid=(t // ROW_TILE,),
        in_specs=[pl.BlockSpec((1, 1, 2 * ROW_TILE), lambda i: (i, 0, 0), memory_space=pltpu.SMEM),
                  pl.BlockSpec(memory_space=pltpu.SMEM), pl.BlockSpec(memory_space=pltpu.SMEM),
                  pl.BlockSpec(memory_space=pl.ANY)],
        out_specs=pl.BlockSpec(memory_space=pl.ANY),
        out_shape=jax.ShapeDtypeStruct((cap,) + ROW_AS_TILE, F32),
        scratch_shapes=[pltpu.VMEM(ring + (ROW_TILE,) + ROW_AS_TILE, F32), pltpu.VMEM((ROW_BLOCK,) + ROW_AS_TILE, F32),
                        pltpu.SemaphoreType.DMA(ring), pltpu.SemaphoreType.DMA(ring), pltpu.SemaphoreType.DMA],
        compiler_params=pltpu.CompilerParams(dimension_semantics=("arbitrary",), has_side_effects=True),
        name="moe_dispatch",
    )(dest3, seg, nused, hn)


def _experts_kernel(be_ref, nused_ref, x_ref, wi_ref, wo_ref, y_ref, wi_bf, wo_bf):
    i = pl.program_id(0)
    used = i < nused_ref[0]
    new_expert = (i == 0) | (be_ref[i] != be_ref[jnp.maximum(i - 1, 0)])

    @pl.when(used & new_expert)
    def _():
        wi_bf[...] = wi_ref[0, 0].astype(BF16)
        wo_bf[...] = wo_ref[0, 0].astype(BF16)

    @pl.when(used)
    def _():
        cols = _tiles_to_cols(x_ref[...])
        x = jnp.concatenate([cols[c] for c in range(SUBLANES)], axis=-1)
        hu = jnp.dot(x.astype(BF16), wi_bf[...], preferred_element_type=F32)
        u = hu[:, :EXPERT_FF]
        act = u * _sigmoid(u) * hu[:, EXPERT_FF:]
        y_ref[...] = _rows_to_tiles(jnp.dot(act.astype(BF16), wo_bf[...], preferred_element_type=F32))

    @pl.when(jnp.logical_not(used))
    def _():
        y_ref[...] = jnp.zeros_like(y_ref)


def _experts(block_expert, nused, xbuf, w_in, w_out, layer):
    cap = xbuf.shape[0]
    grid_spec = pltpu.PrefetchScalarGridSpec(
        num_scalar_prefetch=2,
        grid=(cap // ROW_BLOCK,),
        in_specs=[pl.BlockSpec((ROW_BLOCK,) + ROW_AS_TILE, lambda i, be, nu: (jnp.minimum(i, nu[0] - 1), 0, 0)),
                  pl.BlockSpec((1, 1, D_MODEL, 2 * EXPERT_FF), lambda i, be, nu: (layer, be[i], 0, 0)),
                  pl.BlockSpec((1, 1, EXPERT_FF, D_MODEL), lambda i, be, nu: (layer, be[i], 0, 0))],
        out_specs=pl.BlockSpec((ROW_BLOCK,) + ROW_AS_TILE, lambda i, be, nu: (i, 0, 0)),
        scratch_shapes=[pltpu.VMEM((D_MODEL, 2 * EXPERT_FF), BF16), pltpu.VMEM((EXPERT_FF, D_MODEL), BF16)],
    )
    return pl.pallas_call(
        _experts_kernel,
        grid_spec=grid_spec,
        out_shape=jax.ShapeDtypeStruct((cap,) + ROW_AS_TILE, F32),
        compiler_params=_params("arbitrary"),
        name="moe_experts",
    )(block_expert, nused, xbuf, w_in, w_out)


def _combine_kernel(dest_ref, dest_next_ref, rt_ref, x_ref, y_hbm, o_ref, buf, sem):
    i = pl.program_id(0)
    slot = i % 2

    def gather(dref, s):
        def issue(r, carry):
            for k in range(2):
                pltpu.make_async_copy(y_hbm.at[pl.ds(dref[0, 0, k * ROW_TILE + r], 1)], buf.at[s, k, pl.ds(r, 1)],
                                      sem.at[s]).start(priority=k)
            return carry
        lax.fori_loop(0, ROW_TILE, issue, 0, unroll=ISSUE_UNROLL)

    @pl.when(i == 0)
    def _():
        gather(dest_ref, 0)

    @pl.when(i + 1 < pl.num_programs(0))
    def _():
        gather(dest_next_ref, 1 - slot)

    for k in range(2):
        pltpu.make_async_copy(y_hbm.at[pl.ds(0, ROW_TILE)], buf.at[slot, k], sem.at[slot]).wait()
    rt = rt_ref[...]
    w1, w2 = rt[:, 2:3], rt[:, 3:4]
    g1 = _tiles_to_cols(buf[slot, 0])
    g2 = _tiles_to_cols(buf[slot, 1])
    for c in range(SUBLANES):
        cols = slice(c * LANES, (c + 1) * LANES)
        o_ref[:, cols] = x_ref[:, cols] + w1 * g1[c] + w2 * g2[c]


def _combine(dest3, route, x, ybuf):
    t = x.shape[0]
    nt = t // ROW_TILE
    row = pl.BlockSpec((ROW_TILE, D_MODEL), lambda i: (i, 0))
    return pl.pallas_call(
        _combine_kernel,
        grid=(nt,),
        in_specs=[pl.BlockSpec((1, 1, 2 * ROW_TILE), lambda i: (i, 0, 0), memory_space=pltpu.SMEM),
                  pl.BlockSpec((1, 1, 2 * ROW_TILE), lambda i: (jnp.minimum(i + 1, nt - 1), 0, 0),
                               memory_space=pltpu.SMEM),
                  pl.BlockSpec((ROW_TILE, LANES), lambda i: (i, 0)), row,
                  pl.BlockSpec(memory_space=pl.ANY)],
        out_specs=row,
        out_shape=jax.ShapeDtypeStruct((t, D_MODEL), F32),
        scratch_shapes=[pltpu.VMEM((2, 2, ROW_TILE) + ROW_AS_TILE, F32), pltpu.SemaphoreType.DMA((2,))],
        compiler_params=_params("arbitrary"),
        name="moe_combine",
    )(dest3, dest3, route, x, ybuf)


def _moe(x, hn, route, route_rows, counts, w_in, w_out, layer):
    t = x.shape[0]
    nt = t // ROW_TILE
    n_rb = -(-2 * t // ROW_BLOCK) + N_EXPERTS
    cap = n_rb * ROW_BLOCK
    cnt = counts[:, 0, :N_EXPERTS].astype(jnp.int32)
    total = jnp.sum(cnt, axis=0)
    padded = (total + ROW_BLOCK - 1) // ROW_BLOCK * ROW_BLOCK
    pad_end = jnp.cumsum(padded)
    pad_start = pad_end - padded
    tile_start = pad_start[None, :] + jnp.cumsum(cnt, axis=0) - cnt
    ids = route_rows[0:2].astype(jnp.int32)
    picked = ids[None] == jnp.arange(N_EXPERTS, dtype=jnp.int32)[:, None, None]
    starts = jnp.broadcast_to(tile_start.T[:, :, None], (N_EXPERTS, nt, ROW_TILE)).reshape(N_EXPERTS, 1, t)
    dest = jnp.sum(jnp.where(picked, starts, 0), axis=0) + route_rows[4:6].astype(jnp.int32)
    dest3 = dest.reshape(2, nt, ROW_TILE).transpose(1, 0, 2).reshape(nt, 1, 2 * ROW_TILE)
    seg = jnp.stack([pad_start + total, pad_end]).astype(jnp.int32)
    block_row = jnp.arange(n_rb, dtype=jnp.int32) * ROW_BLOCK
    block_expert = jnp.minimum(jnp.sum(pad_end[None, :] <= block_row[:, None], axis=1), N_EXPERTS - 1).astype(jnp.int32)
    nused = (pad_end[-1:] // ROW_BLOCK).astype(jnp.int32)
    xbuf = _dispatch(dest3, seg, nused, hn, cap)
    ybuf = _experts(block_expert, nused, xbuf, w_in, w_out, layer)
    return _combine(dest3, route, x, ybuf)


def _kvq_kernel(x_ref, kvn_ref, qn_ref, wk_ref, wvt_ref, wqt_ref, knw_ref, qnw_ref, k_ref, vt_ref, qt_ref, km_ref):
    x = x_ref[...]
    per_step = x.shape[0] // MOBA_BLOCK
    hkv = _rms(x, kvn_ref[...]).astype(BF16)
    k = _head_norm(jnp.dot(hkv, wk_ref[...], preferred_element_type=F32), knw_ref[...])
    k_ref[...] = k.astype(k_ref.dtype)
    vt = _dot_nt(wvt_ref[...], hkv)
    hq = _rms(x, qn_ref[...]).astype(BF16)
    qt = _dot_nt(wqt_ref[...], hq)
    qnw = qnw_ref[...]
    for j in range(per_step):
        toks = slice(j * MOBA_BLOCK, (j + 1) * MOBA_BLOCK)
        km_ref[j] = jnp.mean(k[toks], axis=0, keepdims=True)
        for h in range(N_HEADS):
            hs = slice(h * HEAD_DIM, (h + 1) * HEAD_DIM)
            vt_ref[0, h, j, :HEAD_DIM] = vt[hs, toks].astype(vt_ref.dtype)
            vt_ref[0, h, j, HEAD_DIM:] = jnp.ones((V_ROWS - HEAD_DIM, MOBA_BLOCK), vt_ref.dtype)
            seg = qt[hs, toks]
            qt_ref[0, h, j] = seg * lax.rsqrt(jnp.mean(seg * seg, axis=0, keepdims=True) + EPS) * qnw


def _kvq(x, kv_norm, q_prenorm, w_kv, w_q, k_norm, q_norm, batch, nb):
    t = x.shape[0]
    per_step = KV_TILE // MOBA_BLOCK
    assert nb % per_step == 0
    steps = nb // per_step
    row = pl.BlockSpec((KV_TILE, D_MODEL), lambda i: (i, 0))

    def tblk(rows):
        return pl.BlockSpec((1, N_HEADS, per_step, rows, MOBA_BLOCK), lambda i: (i // steps, 0, i % steps, 0, 0))

    def tshape(rows):
        return (batch, N_HEADS, nb, rows, MOBA_BLOCK)

    w_k = w_kv[:, :D_MODEL].astype(BF16)
    w_vt = w_kv[:, D_MODEL:].T.astype(BF16)
    w_qt = w_q.T.astype(BF16)
    qnw = jnp.broadcast_to(q_norm.reshape(HEAD_DIM, 1), (HEAD_DIM, MOBA_BLOCK))
    wspec = _const_spec((D_MODEL, D_MODEL))
    return pl.pallas_call(
        _kvq_kernel,
        grid=(t // KV_TILE,),
        in_specs=[row, _const_spec((1, D_MODEL)), _const_spec((1, D_MODEL)), wspec, wspec, wspec,
                  _const_spec((1, HEAD_DIM)), _const_spec((HEAD_DIM, MOBA_BLOCK))],
        out_specs=[row, tblk(V_ROWS), tblk(HEAD_DIM), pl.BlockSpec((per_step, 1, D_MODEL), lambda i: (i, 0, 0))],
        out_shape=[jax.ShapeDtypeStruct((t, D_MODEL), BF16), jax.ShapeDtypeStruct(tshape(V_ROWS), BF16),
                   jax.ShapeDtypeStruct(tshape(HEAD_DIM), F32),
                   jax.ShapeDtypeStruct((t // MOBA_BLOCK, 1, D_MODEL), F32)],
        compiler_params=_params("parallel"),
        name="kvq",
    )(x, kv_norm, q_prenorm, w_k, w_vt, w_qt, k_norm, qnw)


def _bias_tiles_kernel(rb_ref, o_ref):
    h, delta = pl.program_id(0), pl.program_id(1)
    shape = (MOBA_BLOCK, MOBA_BLOCK)
    dist = delta * MOBA_BLOCK + lax.broadcasted_iota(jnp.int32, shape, 1) - lax.broadcasted_iota(jnp.int32, shape, 0)
    n = jnp.maximum(dist, 0)
    max_exact = NUM_BUCKETS // 2
    log_ratio = (jnp.log(jnp.maximum(n, max_exact).astype(F32) / max_exact)
                 / math.log(MAX_DISTANCE / max_exact))
    large = max_exact + (log_ratio * (NUM_BUCKETS - max_exact)).astype(jnp.int32)
    bucket = jnp.where(n < max_exact, n, jnp.minimum(large, NUM_BUCKETS - 1))
    bias = jnp.zeros(shape, F32)
    for b in range(NUM_BUCKETS):
        bias = jnp.where(bucket == b, rb_ref[b, h], bias)
    o_ref[0, 0] = jnp.where(dist >= 0, bias * LOG2E, NEG)


def _bias_tiles(rel_bias):
    return pl.pallas_call(
        _bias_tiles_kernel,
        grid=(N_HEADS, N_NEAR + 1),
        in_specs=[pl.BlockSpec(memory_space=pltpu.SMEM)],
        out_specs=pl.BlockSpec((1, 1, MOBA_BLOCK, MOBA_BLOCK), lambda h, d: (h, d, 0, 0)),
        out_shape=jax.ShapeDtypeStruct((N_HEADS, N_NEAR + 1, MOBA_BLOCK, MOBA_BLOCK), F32),
        compiler_params=_params("parallel", "parallel"),
        name="bias_tiles",
    )(rel_bias)


def _select_kernel(qt_ref, km_ref, o_ref):
    i = pl.program_id(1)
    nb = km_ref.shape[1]
    km = km_ref[0]
    blk = lax.broadcasted_iota(jnp.int32, (nb, MOBA_BLOCK), 0)
    blkf = blk.astype(F32)
    for h in range(N_HEADS):
        kmh, qth = km[:, h * HEAD_DIM:(h + 1) * HEAD_DIM], qt_ref[0, h, 0]
        k_hi, q_hi = kmh.astype(BF16), qth.astype(BF16)
        k_lo, q_lo = (kmh - k_hi.astype(F32)).astype(BF16), (qth - q_hi.astype(F32)).astype(BF16)
        gate = (jnp.dot(k_hi, q_hi, preferred_element_type=F32) + jnp.dot(k_lo, q_hi, preferred_element_type=F32)
                + jnp.dot(k_hi, q_lo, preferred_element_type=F32))
        g = jnp.where(blk < i, gate, NEG)
        mask = jnp.full((nb, MOBA_BLOCK), NEG, F32)
        for _ in range(MOBA_TOPK):
            m = jnp.max(g, axis=0, keepdims=True)
            idx = jnp.min(jnp.where(g == m, blkf, float(nb)), axis=0, keepdims=True)
            hit = (blkf == idx) & (m > 0.5 * NEG)
            mask = jnp.where(hit, 0.0, mask)
            g = jnp.where(blkf == idx, NEG, g)
        o_ref[0, h, 0] = jnp.where(blk == i, 0.0, mask)


def _select(qt, kmean, batch, nb):
    tblk = pl.BlockSpec((1, N_HEADS, 1, HEAD_DIM, MOBA_BLOCK), lambda b, i: (b, 0, i, 0, 0))
    return pl.pallas_call(
        _select_kernel,
        grid=(batch, nb),
        in_specs=[tblk, pl.BlockSpec((1, nb, D_MODEL), lambda b, i: (b, 0, 0))],
        out_specs=pl.BlockSpec((1, N_HEADS, 1, nb, MOBA_BLOCK), lambda b, i: (b, 0, i, 0, 0)),
        out_shape=jax.ShapeDtypeStruct((batch, N_HEADS, nb, nb, MOBA_BLOCK), F32),
        compiler_params=_params("parallel", "parallel"),
        name="moba_select",
    )(qt, kmean)


def _attention_kernel(qt_ref, k_ref, vt_ref, mask_ref, bias_ref, o_ref, acc_ref, ss0_ref, ss1_ref, ps0_ref, ps1_ref):
    i = pl.program_id(2)
    blk = MOBA_BLOCK
    nbatch = qt_ref.shape[0]
    ss_refs, ps_refs = (ss0_ref, ss1_ref), (ps0_ref, ps1_ref)
    qts = [(qt_ref[b, 0, 0] * (ATT_SCALE * LOG2E)).astype(BF16) for b in range(nbatch)]
    far_bias = bias_ref[0, N_NEAR, 0:1, :]

    def block_of(t):
        near = jnp.where(t < ATT_NEAR_STEPS, i - t, t - ATT_NEAR_STEPS)
        return jnp.where((t == 0) | (t > i), i, near)

    def stage_s(t, slot):
        j = block_of(t)
        for b in range(nbatch):
            ss_refs[slot][b] = jnp.dot(k_ref[b, pl.ds(pl.multiple_of(j * blk, blk), blk), :], qts[b],
                                      preferred_element_type=F32)

    def stage_f(t, slot, ms, near):
        j = block_of(t)
        dead = jnp.where(t > i, NEG, 0.0)
        if near:
            bias = bias_ref[0, jnp.minimum(i - j, N_NEAR)]
        new_m, alphas = [], []
        for b in range(nbatch):
            s = ss_refs[slot][b]
            rowb = mask_ref[b, 0, 0, pl.ds(j, 1), :] + dead
            if near:
                s = s + bias
            else:
                rowb = rowb + far_bias
            mc = jnp.maximum(ms[b], jnp.max(s, axis=0, keepdims=True) + rowb)
            ps_refs[slot][b] = jnp.exp2(s - (mc - rowb)).astype(BF16)
            alphas.append(jnp.exp2(ms[b] - mc))
            new_m.append(mc)
        return tuple(new_m), tuple(alphas)

    def stage_a(t, slot, alphas):
        j = block_of(t)
        pvs = [jnp.dot(vt_ref[b, 0, j], ps_refs[slot][b], preferred_element_type=F32) for b in range(nbatch)]
        for b in range(nbatch):
            acc_ref[b] = alphas[b] * acc_ref[b] + pvs[b]

    stage_s(0, 0)
    ps1_ref[...] = jnp.zeros_like(ps1_ref)
    acc_ref[...] = jnp.zeros_like(acc_ref)
    row = (1, blk)
    init = (tuple(jnp.full(row, NEG, F32) for _ in range(nbatch)), tuple(jnp.ones(row, F32) for _ in range(nbatch)))

    def two_steps(tt, carry, near):
        ms, alphas = carry
        t = 2 * tt
        stage_s(t + 1, 1)
        ms, alphas0 = stage_f(t, 0, ms, near)
        stage_a(jnp.maximum(t - 1, 0), 1, alphas)
        stage_s(t + 2, 0)
        ms, alphas1 = stage_f(t + 1, 1, ms, near)
        stage_a(t, 0, alphas0)
        return ms, alphas1

    trips = (i + 2) // 2
    near_trips = jnp.minimum(trips, ATT_NEAR_STEPS // 2)
    carry = lax.fori_loop(0, near_trips, functools.partial(two_steps, near=True), init)
    ms, alphas = lax.fori_loop(near_trips, trips, functools.partial(two_steps, near=False), carry)
    stage_a(2 * trips - 1, 1, alphas)
    for b in range(nbatch):
        acc = acc_ref[b]
        o_ref[b] = (acc[:HEAD_DIM] / acc[HEAD_DIM:HEAD_DIM + 1]).T.astype(o_ref.dtype)


def _attention(qt, kn, vt, mask, bias, batch, seq):
    nb = seq // MOBA_BLOCK
    bg = max(c for c in range(1, ATT_BATCH_GROUP + 1) if batch % c == 0)
    once = pl.Buffered(2)
    tile = (bg, MOBA_BLOCK, MOBA_BLOCK)
    out = pl.pallas_call(
        _attention_kernel,
        grid=(batch // bg, N_HEADS, nb),
        in_specs=[pl.BlockSpec((bg, 1, 1, HEAD_DIM, MOBA_BLOCK), lambda g, h, i: (g, h, i, 0, 0)),
                  pl.BlockSpec((bg, seq, HEAD_DIM), lambda g, h, i: (g, 0, h), pipeline_mode=once),
                  pl.BlockSpec((bg, 1, nb, V_ROWS, MOBA_BLOCK), lambda g, h, i: (g, h, 0, 0, 0), pipeline_mode=once),
                  pl.BlockSpec((bg, 1, 1, nb, MOBA_BLOCK), lambda g, h, i: (g, h, i, 0, 0)),
                  pl.BlockSpec((1, N_NEAR + 1, MOBA_BLOCK, MOBA_BLOCK), lambda g, h, i: (h, 0, 0, 0))],
        out_specs=pl.BlockSpec((bg, MOBA_BLOCK, HEAD_DIM), lambda g, h, i: (g, i, h)),
        out_shape=jax.ShapeDtypeStruct((batch, seq, D_MODEL), BF16),
        scratch_shapes=[pltpu.VMEM((bg, V_ROWS, MOBA_BLOCK), F32), pltpu.VMEM(tile, F32), pltpu.VMEM(tile, F32), pltpu.VMEM(tile, BF16), pltpu.VMEM(tile, BF16)],
        compiler_params=_params("parallel", "parallel", "arbitrary"),
        name="moba_attention",
    )(qt, kn.reshape(batch, seq, D_MODEL), vt, mask, bias)
    return out.reshape(batch * seq, D_MODEL)


def kernel(x, mix_norm, ffn_norm, hg_w_in, hg_lb, hg_o_norm, hg_w_out, kv_norm, w_kv, k_norm, att_w_q, q_norm, att_w_o, rel_bias, moe_w_group, moe_b_group, moe_w_expert, moe_b_expert, moe_w_in, moe_w_out):
    batch, seq, d = x.shape
    assert d == D_MODEL and seq % GLA_TILE == 0 and seq % MOBA_BLOCK == 0
    assert mix_norm.shape[0] == 2 and hg_w_in.shape[0] == 1 and att_w_q.shape[0] == 1
    x0 = x.reshape(batch * seq, d)

    q, f, v, g = _hg_proj(x0, mix_norm[0:1], hg_lb, hg_w_in[0].astype(BF16))
    og = _gla(q, f, v, g, hg_o_norm[0:1], batch, seq)
    x1, hn, route, route_rows, counts = _proj_route(
        og, x0, hg_w_out[0].astype(BF16), ffn_norm[0:1],
        *_router_weights(moe_w_group[0], moe_b_group[0], moe_w_expert[0], moe_b_expert[0]))
    x2 = _moe(x1, hn, route, route_rows, counts, moe_w_in, moe_w_out, 0)

    nb = seq // MOBA_BLOCK
    kn, vt, qt, kmean = _kvq(x2, kv_norm[None, :], mix_norm[1:2], w_kv, att_w_q[0], k_norm[None, :], q_norm[0],
                             batch, nb)
    mask = _select(qt, kmean.reshape(batch, nb, d), batch, nb)
    att = _attention(qt, kn, vt, mask, _bias_tiles(rel_bias), batch, seq)
    x3, hn, route, route_rows, counts = _proj_route(
        att, x2, att_w_o[0].astype(BF16), ffn_norm[1:2],
        *_router_weights(moe_w_group[1], moe_b_group[1], moe_w_expert[1], moe_b_expert[1]))
    out = _moe(x3, hn, route, route_rows, counts, moe_w_in, moe_w_out, 1)
    return out.reshape(batch, seq, d)
```
